```python
import jax, jax.numpy as jnp
from jax import lax
import numpy as np

D_MODEL = 1024
BATCH = 4
SEQ = 8192
DEPTH = 1

N_MEM = 256
EPS = 1e-6
MIX_WIDTH = D_MODEL
ATTN_WIDTH = MIX_WIDTH // 2
ATTN_HEAD_DIM = 64
ATTN_Q_HEADS = ATTN_WIDTH // ATTN_HEAD_DIM
ATTN_KV_HEADS = ATTN_Q_HEADS // 4
ATTN_KV_WIDTH = ATTN_KV_HEADS * ATTN_HEAD_DIM
WINDOW = 128
BLOCK = 128
HGRN_WIDTH = MIX_WIDTH - ATTN_WIDTH
HGRN_VAL_DIM = 128
HGRN_HEADS = HGRN_WIDTH // HGRN_VAL_DIM
HGRN_KEY_DIM = 128
HGRN_FDIM = HGRN_HEADS * HGRN_KEY_DIM
CHUNK = 64
IN_SPLITS = (ATTN_WIDTH, ATTN_KV_WIDTH, ATTN_KV_WIDTH, HGRN_FDIM, HGRN_FDIM, HGRN_WIDTH, HGRN_WIDTH)
IN_PROJ_WIDTH = sum(IN_SPLITS)
CA_HEADS = 4
CA_HEAD_DIM = D_MODEL // CA_HEADS
CA_WIDTH = CA_HEADS * CA_HEAD_DIM
D_FF = 2816
CONV_WIDTH = 3

kernel_name = "hybrid_swa_sink_hgrn2_memxattn_convffn"


def rms_norm(x, w):
    xf = x.astype(jnp.float32)
    y = xf * lax.rsqrt(jnp.mean(xf * xf, axis=-1, keepdims=True) + EPS)
    return (y * w.astype(jnp.float32)).astype(x.dtype)


def sliding_window_sink_attention(q, k, v, sinks):
    B, T, Hq, D = q.shape
    Hkv = k.shape[2]
    G = Hq // Hkv
    nb = T // BLOCK
    qb = q.reshape(B, nb, BLOCK, Hkv, G, D)

    def with_prev(t):
        tb = t.reshape(B, nb, BLOCK, Hkv, D)
        prev = jnp.pad(tb, ((0, 0), (1, 0), (0, 0), (0, 0), (0, 0)))[:, :-1]
        return jnp.concatenate([prev, tb], axis=2)

    kw, vw = with_prev(k), with_prev(v)
    s = jnp.einsum('bnqhgd,bnkhd->bnhgqk', qb, kw).astype(jnp.float32) * (D ** -0.5)
    qi = jnp.arange(BLOCK)[:, None]
    kj = jnp.arange(2 * BLOCK)[None, :]
    diff = qi + BLOCK - kj
    key_pos = jnp.arange(nb)[:, None, None] * BLOCK - BLOCK + kj[None]
    allowed = (diff >= 0) & (diff < WINDOW) & (key_pos >= 0)
    s = jnp.where(allowed[None, :, None, None], s, -jnp.inf)
    sink = sinks.astype(jnp.float32).reshape(Hkv, G)[None, None, :, :, None, None]
    sink = jnp.broadcast_to(sink, s.shape[:-1] + (1,))
    p = jax.nn.softmax(jnp.concatenate([s, sink], axis=-1), axis=-1)[..., :-1]
    o = jnp.einsum('bnhgqk,bnkhd->bnqhgd', p.astype(v.dtype), vw)
    return o.reshape(B, T, Hq * D)


def hgrn2_chunkwise(q, k, v, log_f):
    B, T, H, K = q.shape
    V = v.shape[-1]
    n = T // CHUNK

    def to_chunks(t):
        return t.reshape(B, n, CHUNK, H, t.shape[-1]).transpose(1, 0, 3, 2, 4)

    causal = jnp.tril(jnp.ones((CHUNK, CHUNK), dtype=bool))

    def step(S, xs):
        qc, kc, vc, gc = xs
        bc = jnp.cumsum(gc, axis=2)
        rel = bc[:, :, :, None, :] - bc[:, :, None, :, :]
        decay = jnp.exp(jnp.where(causal[:, :, None], rel, -jnp.inf))
        A = jnp.einsum('bhtk,bhsk,bhtsk->bhts', qc, kc, decay)
        o = jnp.einsum('bhts,bhsv->bhtv', A, vc) + jnp.einsum('bhtk,bhkv->bhtv', qc * jnp.exp(bc), S)
        b_last = bc[:, :, -1:, :]
        S = S * jnp.exp(b_last[:, :, 0, :])[..., None] + jnp.einsum(
            'bhsk,bhsv->bhkv', kc * jnp.exp(b_last - bc), vc)
        return S, o

    S0 = jnp.zeros((B, H, K, V), jnp.float32)
    _, o = lax.scan(step, S0, (to_chunks(q), to_chunks(k), to_chunks(v), to_chunks(log_f)))
    return o.transpose(1, 0, 3, 2, 4).reshape(B, T, H, V)


def hgrn2_group(q_raw, f_raw, i_raw, g_raw, lb, out_norm_w):
    B, T, _ = q_raw.shape
    f32 = jnp.float32
    q = jax.nn.silu(q_raw.astype(f32)).reshape(B, T, HGRN_HEADS, HGRN_KEY_DIM) * (HGRN_KEY_DIM ** -0.5)
    fr = f_raw.astype(f32)
    lb = lb.astype(f32)
    f = lb + (1.0 - lb) * jax.nn.sigmoid(fr)
    k = (1.0 - lb) * jax.nn.sigmoid(-fr)
    log_f = jnp.log(f)
    k = k.reshape(B, T, HGRN_HEADS, HGRN_KEY_DIM)
    log_f = log_f.reshape(B, T, HGRN_HEADS, HGRN_KEY_DIM)
    v = i_raw.astype(f32).reshape(B, T, HGRN_HEADS, HGRN_VAL_DIM)
    o = hgrn2_chunkwise(q, k, v, log_f)
    o = rms_norm(o, out_norm_w).reshape(B, T, HGRN_WIDTH)
    return (o * jax.nn.silu(g_raw.astype(f32))).astype(q_raw.dtype)


def memory_cross_attention(h, mem_n, wq, wk, wv, wo):
    B, T, _ = h.shape
    M = mem_n.shape[1]
    q = (h @ wq).reshape(B, T, CA_HEADS, CA_HEAD_DIM)
    k = (mem_n @ wk).reshape(B, M, CA_HEADS, CA_HEAD_DIM)
    v = (mem_n @ wv).reshape(B, M, CA_HEADS, CA_HEAD_DIM)
    s = jnp.einsum('bthd,bmhd->bhtm', q, k).astype(jnp.float32) * (CA_HEAD_DIM ** -0.5)
    p = jax.nn.softmax(s, axis=-1).astype(v.dtype)
    o = jnp.einsum('bhtm,bmhd->bthd', p, v).reshape(B, T, CA_WIDTH)
    return o @ wo


def conv_ffn(h, w_up, conv_w, conv_b, w_down):
    u = h @ w_up
    C = u.shape[-1]
    u = lax.conv_general_dilated(
        u, conv_w.reshape(CONV_WIDTH, 1, C).astype(u.dtype), window_strides=(1,),
        padding=[(CONV_WIDTH - 1, 0)], dimension_numbers=('NWC', 'WIO', 'NWC'),
        feature_group_count=C) + conv_b
    gate, val = jnp.split(u, 2, axis=-1)
    return (jax.nn.gelu(gate, approximate=True) * val) @ w_down


def setup_inputs(seed: int = 0) -> dict:
    key = jax.random.key(seed)
    ks = jax.random.split(key, 24)
    f32 = jnp.float32

    def w(k, shape, fan_in):
        return jax.random.normal(k, shape, f32) * (fan_in ** -0.5)

    def gain(k, shape):
        return 1.0 + 0.01 * jax.random.normal(k, shape, f32)

    return {
        "x": jax.random.normal(ks[0], (BATCH, SEQ, D_MODEL), f32),
        "mem": jax.random.normal(ks[1], (BATCH, N_MEM, D_MODEL), f32),
        "mix_pre_norm": gain(ks[2], (DEPTH, D_MODEL)),
        "w_in": w(ks[3], (DEPTH, D_MODEL, IN_PROJ_WIDTH), D_MODEL),
        "attn_sinks": 0.5 * jax.random.normal(ks[4], (DEPTH, ATTN_Q_HEADS), f32),
        "hgrn_lb_logits": 0.1 * jax.random.normal(ks[5], (DEPTH + 1, HGRN_FDIM), f32),
        "hgrn_out_norm": gain(ks[6], (DEPTH, HGRN_VAL_DIM)),
        "w_out": w(ks[7], (DEPTH, MIX_WIDTH, D_MODEL), MIX_WIDTH),
        "mix_post_norm": gain(ks[8], (DEPTH, D_MODEL)),
        "ca_pre_norm": gain(ks[9], (DEPTH, D_MODEL)),
        "mem_norm": gain(ks[10], (DEPTH, D_MODEL)),
        "ca_wq": w(ks[11], (DEPTH, D_MODEL, CA_WIDTH), D_MODEL),
        "ca_wk": w(ks[12], (DEPTH, D_MODEL, CA_WIDTH), D_MODEL),
        "ca_wv": w(ks[13], (DEPTH, D_MODEL, CA_WIDTH), D_MODEL),
        "ca_wo": w(ks[14], (DEPTH, CA_WIDTH, D_MODEL), CA_WIDTH),
        "ca_post_norm": gain(ks[15], (DEPTH, D_MODEL)),
        "ffn_pre_norm": gain(ks[16], (DEPTH, D_MODEL)),
        "ffn_w_up": w(ks[17], (DEPTH, D_MODEL, 2 * D_FF), D_MODEL),
        "ffn_conv_w": w(ks[18], (DEPTH, CONV_WIDTH, 2 * D_FF), CONV_WIDTH),
        "ffn_conv_b": 0.01 * jax.random.normal(ks[19], (DEPTH, 2 * D_FF), f32),
        "ffn_w_down": w(ks[20], (DEPTH, D_FF, D_MODEL), D_FF),
        "ffn_post_norm": gain(ks[21], (DEPTH, D_MODEL)),
    }


def reference(x, mem, mix_pre_norm, w_in, attn_sinks, hgrn_lb_logits, hgrn_out_norm, w_out,
              mix_post_norm, ca_pre_norm, mem_norm, ca_wq, ca_wk, ca_wv, ca_wo, ca_post_norm,
              ffn_pre_norm, ffn_w_up, ffn_conv_w, ffn_conv_b, ffn_w_down, ffn_post_norm):
    B, T, _ = x.shape
    lower_bounds = jnp.cumsum(jax.nn.softmax(hgrn_lb_logits.astype(jnp.float32), axis=0), axis=0)
    split_points = list(np.cumsum(IN_SPLITS)[:-1])
    for l in range(DEPTH):
        h = rms_norm(x, mix_pre_norm[l])
        z = h @ w_in[l]
        q_a, k_a, v_a, q_h, f_h, i_h, g_h = jnp.split(z, split_points, axis=-1)
        attn = sliding_window_sink_attention(
            q_a.reshape(B, T, ATTN_Q_HEADS, ATTN_HEAD_DIM),
            k_a.reshape(B, T, ATTN_KV_HEADS, ATTN_HEAD_DIM),
            v_a.reshape(B, T, ATTN_KV_HEADS, ATTN_HEAD_DIM),
            attn_sinks[l])
        rec = hgrn2_group(q_h, f_h, i_h, g_h, lower_bounds[l], hgrn_out_norm[l])
        m = jnp.concatenate([attn.astype(x.dtype), rec.astype(x.dtype)], axis=-1) @ w_out[l]
        x = x + rms_norm(m, mix_post_norm[l])
        h = rms_norm(x, ca_pre_norm[l])
        mem_n = rms_norm(mem, mem_norm[l])
        c = memory_cross_attention(h, mem_n, ca_wq[l], ca_wk[l], ca_wv[l], ca_wo[l])
        x = x + rms_norm(c, ca_post_norm[l])
        h = rms_norm(x, ffn_pre_norm[l])
        y = conv_ffn(h, ffn_w_up[l], ffn_conv_w[l], ffn_conv_b[l], ffn_w_down[l])
        x = x + rms_norm(y, ffn_post_norm[l])
    return x
```

```python
import functools

import jax
import jax.numpy as jnp
from jax import lax
from jax.experimental import pallas as pl
from jax.experimental.pallas import tpu as pltpu

F32 = jnp.float32
BF16 = jnp.bfloat16

EPS = 1e-6
D_MODEL = 1024
ATTN_WIDTH = 512
ATTN_HEAD_DIM = 64
ATTN_GROUP = 4
ATTN_KV_HEADS = 2
ATTN_KV_WIDTH = ATTN_KV_HEADS * ATTN_HEAD_DIM
ATTN_BLOCK = 128
HGRN_HEADS = 4
HGRN_DIM = 128
HGRN_WIDTH = HGRN_HEADS * HGRN_DIM
COL_Q = 0
COL_K = COL_Q + ATTN_WIDTH
COL_V = COL_K + ATTN_KV_WIDTH
COL_HQ = COL_V + ATTN_KV_WIDTH
COL_HF = COL_HQ + HGRN_WIDTH
COL_HI = COL_HF + HGRN_WIDTH
COL_HG = COL_HI + HGRN_WIDTH
IN_PROJ_WIDTH = COL_HG + HGRN_WIDTH
HGRN_CHUNK = 64
HGRN_SUB = 16
CA_HEADS = 4
CA_HEAD_DIM = 256
D_FF = 2816
FF_CHUNK = 256
HALO = 8

TM_MIX = 256
TM_CA = 256
TM_FFN = 256

VMEM_LIMIT = 56 * 1024 * 1024

NT_DIMS = (((1,), (1,)), ((), ()))
TN_DIMS = (((0,), (0,)), ((), ()))


def _rms(x, w):
    return x * lax.rsqrt(jnp.mean(x * x, axis=-1, keepdims=True) + EPS) * w


def _dot(a, b):
    return jnp.dot(a, b, preferred_element_type=F32)


def _dot_nt(a, b):
    return lax.dot_general(a, b, NT_DIMS, preferred_element_type=F32)


def _dot_tn(a, b):
    return lax.dot_general(a, b, TN_DIMS, preferred_element_type=F32)


def _mem_kv_kernel(mem_ref, nw_ref, wk_ref, wv_ref, k_ref, v_ref):
    mn = _rms(mem_ref[0], nw_ref[...]).astype(BF16)
    k_ref[0] = _dot(mn, wk_ref[...]).astype(BF16)
    v_ref[0] = _dot(mn, wv_ref[...]).astype(BF16)


def _mem_kv(mem, norm_w, wk, wv):
    b, m, d = mem.shape
    full = lambda shape: pl.BlockSpec(shape, lambda i: (0,) * len(shape))
    return pl.pallas_call(
        _mem_kv_kernel,
        grid=(b,),
        in_specs=[pl.BlockSpec((1, m, d), lambda i: (i, 0, 0)), full((1, d)), full(wk.shape), full(wv.shape)],
        out_specs=[pl.BlockSpec((1, m, d), lambda i: (i, 0, 0))] * 2,
        out_shape=[jax.ShapeDtypeStruct((b, m, d), BF16)] * 2,
        compiler_params=pltpu.CompilerParams(dimension_semantics=("arbitrary",), vmem_limit_bytes=VMEM_LIMIT),
        name="mem_kv",
    )(mem, norm_w, wk, wv)


def _swa_block(q, kk, vv, sink_ref, first_block):
    nq = ATTN_BLOCK
    lane = lax.broadcasted_iota(jnp.int32, kk.shape, 1)
    k_sw = pltpu.roll(kk, ATTN_HEAD_DIM, axis=1)
    v_sw = pltpu.roll(vv, ATTN_HEAD_DIM, axis=1)
    low = lane < ATTN_HEAD_DIM

    qi = lax.broadcasted_iota(jnp.int32, (nq, 2 * nq), 0)
    kj = lax.broadcasted_iota(jnp.int32, (nq, 2 * nq), 1)
    diff = qi + nq - kj
    allowed = (diff >= 0) & (diff < nq) & ((kj >= nq) | jnp.logical_not(first_block))
    allowed = jnp.concatenate([allowed] * ATTN_GROUP, axis=0)

    gw = ATTN_GROUP * ATTN_HEAD_DIM
    head_of_lane = lax.broadcasted_iota(jnp.int32, (nq, gw), 1) // ATTN_HEAD_DIM
    outs = []
    for j in range(ATTN_KV_HEADS):
        kj2 = jnp.where(low, kk, k_sw) if j == 0 else jnp.where(low, k_sw, kk)
        vj2 = jnp.where(low, vv, v_sw) if j == 0 else jnp.where(low, v_sw, vv)
        k_rep = jnp.concatenate([kj2, kj2], axis=1).astype(BF16)
        v_rep = jnp.concatenate([vj2, vj2], axis=1).astype(BF16)
        qg = q[:, j * gw:(j + 1) * gw]
        qs = jnp.concatenate([jnp.where(head_of_lane == h, qg, 0.0) for h in range(ATTN_GROUP)], axis=0).astype(BF16)
        s = _dot_nt(qs, k_rep)
        s = jnp.where(allowed, s, -jnp.inf)
        sink = jnp.concatenate(
            [jnp.full((nq, 1), sink_ref[j * ATTN_GROUP + h], F32) for h in range(ATTN_GROUP)], axis=0)
        m = jnp.maximum(jnp.max(s, axis=-1, keepdims=True), sink)
        p = jnp.exp(s - m)
        denom = jnp.sum(p, axis=-1, keepdims=True) + jnp.exp(sink - m)
        p = (p * (1.0 / denom)).astype(BF16)
        o = _dot(p, v_rep)
        og = jnp.where(head_of_lane == 0, o[0:nq], 0.0)
        for h in range(1, ATTN_GROUP):
            og = og + jnp.where(head_of_lane == h, o[h * nq:(h + 1) * nq], 0.0)
        outs.append(og)
    return jnp.concatenate(outs, axis=1)


def _hgrn_chunk(qr, fr, iv, gr, lb, onw, s_ref):
    c = HGRN_CHUNK
    q = qr * jax.nn.sigmoid(qr) * (HGRN_DIM ** -0.5)
    f = lb + (1.0 - lb) * jax.nn.sigmoid(fr)
    k = (1.0 - lb) * jax.nn.sigmoid(-fr)
    g = jnp.log(f)

    ti = lax.broadcasted_iota(jnp.int32, (c, c), 0)
    si = lax.broadcasted_iota(jnp.int32, (c, c), 1)
    tril = (si <= ti)
    ltri = tril.astype(BF16)
    g_hi = g.astype(BF16)
    g_lo = (g - g_hi.astype(F32)).astype(BF16)
    bc = _dot(ltri, g_hi) + _dot(ltri, g_lo)

    row = lax.broadcasted_iota(jnp.int32, (c, HGRN_WIDTH), 0)
    p_blocks = [[] for _ in range(HGRN_HEADS)]
    for i in range(c // HGRN_SUB):
        r0 = i * HGRN_SUB
        anchor = bc[r0 + HGRN_SUB // 2 - 1:r0 + HGRN_SUB // 2, :]
        qe = (q[r0:r0 + HGRN_SUB] * jnp.exp(bc[r0:r0 + HGRN_SUB] - anchor)).astype(BF16)
        ke = (k * jnp.exp(jnp.where(row < r0 + HGRN_SUB, anchor - bc, -jnp.inf))).astype(BF16)
        for h in range(HGRN_HEADS):
            hs = slice(h * HGRN_DIM, (h + 1) * HGRN_DIM)
            p_blocks[h].append(_dot_nt(qe[:, hs], ke[:, hs]))

    qd = (q * jnp.exp(bc)).astype(BF16)
    b_last = bc[c - 1:c, :]
    kd = (k * jnp.exp(b_last - bc)).astype(BF16)
    s_decay = jnp.exp(b_last)
    v_b = iv.astype(BF16)
    gate = gr * jax.nn.sigmoid(gr)

    outs = []
    for h in range(HGRN_HEADS):
        hs = slice(h * HGRN_DIM, (h + 1) * HGRN_DIM)
        a = jnp.where(tril, jnp.concatenate(p_blocks[h], axis=0), 0.0).astype(BF16)
        st = s_ref[h]
        o = _dot(a, v_b[:, hs]) + _dot_nt(qd[:, hs], st.astype(BF16))
        s_ref[h] = st * s_decay[:, hs] + _dot_tn(v_b[:, hs], kd[:, hs])
        o = o * lax.rsqrt(jnp.mean(o * o, axis=-1, keepdims=True) + EPS) * onw
        outs.append(o * gate[:, hs])
    return jnp.concatenate(outs, axis=1)


def _mixer_kernel(x_ref, prew_ref, win_ref, sink_ref, lbl_ref, onw_ref, wout_ref, postw_ref, o_ref,
                  z_ref, kvp_ref, s_ref, mix_ref):
    n = pl.program_id(1)
    tm = x_ref.shape[1]

    @pl.when(n == 0)
    def _():
        kvp_ref[...] = jnp.zeros_like(kvp_ref)
        s_ref[...] = jnp.zeros_like(s_ref)

    x = x_ref[0]
    h = _rms(x, prew_ref[...]).astype(BF16)
    z_ref[...] = _dot(h, win_ref[...])

    for blk in range(tm // ATTN_BLOCK):
        r0 = blk * ATTN_BLOCK
        q = z_ref[r0:r0 + ATTN_BLOCK, COL_Q:COL_Q + ATTN_WIDTH] * (ATTN_HEAD_DIM ** -0.5)
        k_cur = z_ref[r0:r0 + ATTN_BLOCK, COL_K:COL_K + ATTN_KV_WIDTH]
        v_cur = z_ref[r0:r0 + ATTN_BLOCK, COL_V:COL_V + ATTN_KV_WIDTH]
        if blk == 0:
            k_prev = kvp_ref[:, 0:ATTN_KV_WIDTH]
            v_prev = kvp_ref[:, ATTN_KV_WIDTH:2 * ATTN_KV_WIDTH]
            first_block = n == 0
        else:
            k_prev = z_ref[r0 - ATTN_BLOCK:r0, COL_K:COL_K + ATTN_KV_WIDTH]
            v_prev = z_ref[r0 - ATTN_BLOCK:r0, COL_V:COL_V + ATTN_KV_WIDTH]
            first_block = jnp.bool_(False)
        kk = jnp.concatenate([k_prev, k_cur], axis=0)
        vv = jnp.concatenate([v_prev, v_cur], axis=0)
        mix_ref[r0:r0 + ATTN_BLOCK, 0:ATTN_WIDTH] = _swa_block(q, kk, vv, sink_ref, first_block).astype(BF16)
    kvp_ref[...] = z_ref[tm - ATTN_BLOCK:tm, COL_K:COL_V + ATTN_KV_WIDTH]

    l0 = lbl_ref[0:1, :]
    l1 = lbl_ref[1:2, :]
    lm = jnp.maximum(l0, l1)
    e0 = jnp.exp(l0 - lm)
    lb = e0 / (e0 + jnp.exp(l1 - lm))

    for ch in range(tm // HGRN_CHUNK):
        r0 = ch * HGRN_CHUNK
        rows = slice(r0, r0 + HGRN_CHUNK)
        rec = _hgrn_chunk(z_ref[rows, COL_HQ:COL_HQ + HGRN_WIDTH], z_ref[rows, COL_HF:COL_HF + HGRN_WIDTH],
                          z_ref[rows, COL_HI:COL_HI + HGRN_WIDTH], z_ref[rows, COL_HG:COL_HG + HGRN_WIDTH],
                          lb, onw_ref[...], s_ref)
        mix_ref[rows, ATTN_WIDTH:ATTN_WIDTH + HGRN_WIDTH] = rec.astype(BF16)

    m = _dot(mix_ref[...], wout_ref[...])
    o_ref[0] = x + _rms(m, postw_ref[...])


def _mixer(x, pre_w, w_in, sinks, lb_logits, out_norm_w, w_out, post_w):
    b, t, d = x.shape
    tm = TM_MIX
    assert t % tm == 0 and d == D_MODEL and w_in.shape == (D_MODEL, IN_PROJ_WIDTH)
    full = lambda shape: pl.BlockSpec(shape, lambda i, j: (0,) * len(shape))
    return pl.pallas_call(
        _mixer_kernel,
        grid=(b, t // tm),
        in_specs=[
            pl.BlockSpec((1, tm, d), lambda i, j: (i, j, 0)),
            full((1, d)),
            full(w_in.shape),
            pl.BlockSpec(memory_space=pltpu.SMEM),
            full(lb_logits.shape),
            full((1, HGRN_DIM)),
            full(w_out.shape),
            full((1, d)),
        ],
        out_specs=pl.BlockSpec((1, tm, d), lambda i, j: (i, j, 0)),
        out_shape=jax.ShapeDtypeStruct((b, t, d), F32),
        scratch_shapes=[
            pltpu.VMEM((tm, IN_PROJ_WIDTH), F32),
            pltpu.VMEM((ATTN_BLOCK, 2 * ATTN_KV_WIDTH), F32),
            pltpu.VMEM((HGRN_HEADS, HGRN_DIM, HGRN_DIM), F32),
            pltpu.VMEM((tm, ATTN_WIDTH + HGRN_WIDTH), BF16),
        ],
        compiler_params=pltpu.CompilerParams(
            dimension_semantics=("arbitrary", "arbitrary"), vmem_limit_bytes=VMEM_LIMIT),
        name="mixer",
    )(x, pre_w, w_in, sinks, lb_logits, out_norm_w, w_out, post_w)


def _xattn_kernel(x_ref, prew_ref, wq_ref, k_ref, v_ref, wo_ref, postw_ref, o_ref, cat_ref):
    x = x_ref[0]
    h = _rms(x, prew_ref[...]).astype(BF16)
    q = (_dot(h, wq_ref[...]) * (CA_HEAD_DIM ** -0.5)).astype(BF16)
    for hd in range(CA_HEADS):
        hs = slice(hd * CA_HEAD_DIM, (hd + 1) * CA_HEAD_DIM)
        s = _dot_nt(q[:, hs], k_ref[0, :, hs])
        m = jnp.max(s, axis=-1, keepdims=True)
        p = jnp.exp(s - m)
        p = (p * (1.0 / jnp.sum(p, axis=-1, keepdims=True))).astype(BF16)
        cat_ref[:, hs] = _dot(p, v_ref[0, :, hs]).astype(BF16)
    c = _dot(cat_ref[...], wo_ref[...])
    o_ref[0] = x + _rms(c, postw_ref[...])


def _xattn(x, pre_w, wq, k, v, wo, post_w):
    b, t, d = x.shape
    tm = TM_CA
    nm = k.shape[1]
    assert t % tm == 0
    full = lambda shape: pl.BlockSpec(shape, lambda i, j: (0,) * len(shape))
    return pl.pallas_call(
        _xattn_kernel,
        grid=(b, t // tm),
        in_specs=[
            pl.BlockSpec((1, tm, d), lambda i, j: (i, j, 0)),
            full((1, d)),
            full(wq.shape),
            pl.BlockSpec((1, nm, d), lambda i, j: (i, 0, 0)),
            pl.BlockSpec((1, nm, d), lambda i, j: (i, 0, 0)),
            full(wo.shape),
            full((1, d)),
        ],
        out_specs=pl.BlockSpec((1, tm, d), lambda i, j: (i, j, 0)),
        out_shape=jax.ShapeDtypeStruct((b, t, d), F32),
        scratch_shapes=[pltpu.VMEM((tm, d), BF16)],
        compiler_params=pltpu.CompilerParams(
            dimension_semantics=("arbitrary", "arbitrary"), vmem_limit_bytes=VMEM_LIMIT),
        name="xattn",
    )(x, pre_w, wq, k, v, wo, post_w)


def _gelu_tanh(x):
    return 0.5 * x * (1.0 + jnp.tanh(0.7978845608028654 * (x + 0.044715 * (x * x * x))))


def _causal_conv3(u, halo, w, b):
    tm = u.shape[0]
    ext = jnp.concatenate([halo, u], axis=0)
    return (w[2:3] * u + w[1:2] * ext[HALO - 1:HALO - 1 + tm] + w[0:1] * ext[HALO - 2:HALO - 2 + tm]) + b


def _ffn_kernel(x_ref, prew_ref, wup_ref, cw_ref, cb_ref, wdn_ref, postw_ref, o_ref, halo_ref):
    n = pl.program_id(1)
    tm = x_ref.shape[1]

    @pl.when(n == 0)
    def _():
        halo_ref[...] = jnp.zeros_like(halo_ref)

    x = x_ref[0]
    h = _rms(x, prew_ref[...]).astype(BF16)
    acc = jnp.zeros((tm, D_MODEL), F32)
    for c in range(D_FF // FF_CHUNK):
        gs = slice(c * FF_CHUNK, (c + 1) * FF_CHUNK)
        vs = slice(D_FF + c * FF_CHUNK, D_FF + (c + 1) * FF_CHUNK)
        ug = _dot(h, wup_ref[:, gs])
        uv = _dot(h, wup_ref[:, vs])
        gate = _causal_conv3(ug, halo_ref[:, gs], cw_ref[:, gs], cb_ref[:, gs])
        val = _causal_conv3(uv, halo_ref[:, vs], cw_ref[:, vs], cb_ref[:, vs])
        halo_ref[:, gs] = ug[tm - HALO:tm]
        halo_ref[:, vs] = uv[tm - HALO:tm]
        act = (_gelu_tanh(gate) * val).astype(BF16)
        acc = acc + _dot(act, wdn_ref[gs, :])
    o_ref[0] = x + _rms(acc, postw_ref[...])


def _ffn(x, pre_w, w_up, conv_w, conv_b, w_down, post_w):
    b, t, d = x.shape
    tm = TM_FFN
    assert t % tm == 0 and w_up.shape == (D_MODEL, 2 * D_FF) and D_FF % FF_CHUNK == 0
    full = lambda shape: pl.BlockSpec(shape, lambda i, j: (0,) * len(shape))
    return pl.pallas_call(
        _ffn_kernel,
        grid=(b, t // tm),
        in_specs=[
            pl.BlockSpec((1, tm, d), lambda i, j: (i, j, 0)),
            full((1, d)),
            full(w_up.shape),
            full(conv_w.shape),
            full((1, 2 * D_FF)),
            full(w_down.shape),
            full((1, d)),
        ],
        out_specs=pl.BlockSpec((1, tm, d), lambda i, j: (i, j, 0)),
        out_shape=jax.ShapeDtypeStruct((b, t, d), F32),
        scratch_shapes=[pltpu.VMEM((HALO, 2 * D_FF), F32)],
        compiler_params=pltpu.CompilerParams(
            dimension_semantics=("arbitrary", "arbitrary"), vmem_limit_bytes=VMEM_LIMIT),
        name="ffn",
    )(x, pre_w, w_up, conv_w, conv_b, w_down, post_w)


def kernel(x, mem, mix_pre_norm, w_in, attn_sinks, hgrn_lb_logits, hgrn_out_norm, w_out, mix_post_norm, ca_pre_norm,
           mem_norm, ca_wq, ca_wk, ca_wv, ca_wo, ca_post_norm, ffn_pre_norm, ffn_w_up, ffn_conv_w, ffn_conv_b,
           ffn_w_down, ffn_post_norm):
    depth = w_in.shape[0]
    assert depth == 1 and hgrn_lb_logits.shape[0] == 2
    bf = lambda w: w.astype(BF16)
    for l in range(depth):
        x = _mixer(x, mix_pre_norm[l:l + 1], bf(w_in[l]), attn_sinks[l], hgrn_lb_logits, hgrn_out_norm[l:l + 1],
                   bf(w_out[l]), mix_post_norm[l:l + 1])
        k, v = _mem_kv(mem, mem_norm[l:l + 1], bf(ca_wk[l]), bf(ca_wv[l]))
        x = _xattn(x, ca_pre_norm[l:l + 1], bf(ca_wq[l]), k, v, bf(ca_wo[l]), ca_post_norm[l:l + 1])
        x = _ffn(x, ffn_pre_norm[l:l + 1], bf(ffn_w_up[l]), ffn_conv_w[l], ffn_conv_b[l:l + 1], bf(ffn_w_down[l]),
                 ffn_post_norm[l:l + 1])
    return x
```

```python
import functools

import jax
import jax.numpy as jnp
from jax import lax
from jax.experimental import pallas as pl
from jax.experimental.pallas import tpu as pltpu

F32 = jnp.float32
BF16 = jnp.bfloat16

EPS = 1e-6
D_MODEL = 1024
ATTN_WIDTH = 512
ATTN_HEAD_DIM = 64
ATTN_GROUP = 4
ATTN_KV_HEADS = 2
ATTN_KV_WIDTH = ATTN_KV_HEADS * ATTN_HEAD_DIM
ATTN_BLOCK = 128
HGRN_HEADS = 4
HGRN_DIM = 128
HGRN_WIDTH = HGRN_HEADS * HGRN_DIM
COL_Q = 0
COL_K = COL_Q + ATTN_WIDTH
COL_V = COL_K + ATTN_KV_WIDTH
COL_HQ = COL_V + ATTN_KV_WIDTH
COL_HF = COL_HQ + HGRN_WIDTH
COL_HI = COL_HF + HGRN_WIDTH
COL_HG = COL_HI + HGRN_WIDTH
IN_PROJ_WIDTH = COL_HG + HGRN_WIDTH
HGRN_CHUNK = 64
HGRN_SUB = 16
CA_HEADS = 4
CA_HEAD_DIM = 256
D_FF = 2816
FF_CHUNK = 256
HALO = 8
LANES = 128

TM_MIX = 256
TM_CA = 256
TM_FFN = 256

VMEM_LIMIT = 56 * 1024 * 1024

NT_DIMS = (((1,), (1,)), ((), ()))
TN_DIMS = (((0,), (0,)), ((), ()))


def _rms(x, w):
    return x * lax.rsqrt(jnp.mean(x * x, axis=-1, keepdims=True) + EPS) * w


def _dot(a, b):
    return jnp.dot(a, b, preferred_element_type=F32)


def _dot_nt(a, b):
    return lax.dot_general(a, b, NT_DIMS, preferred_element_type=F32)


def _dot_tn(a, b):
    return lax.dot_general(a, b, TN_DIMS, preferred_element_type=F32)


def _mem_kv_kernel(mem_ref, nw_ref, wk_ref, wv_ref, k_ref, v_ref):
    mn = _rms(mem_ref[0], nw_ref[...]).astype(BF16)
    k_ref[0] = _dot(mn, wk_ref[...]).astype(BF16)
    v_ref[0] = _dot(mn, wv_ref[...]).astype(BF16)


def _mem_kv(mem, norm_w, wk, wv):
    b, m, d = mem.shape
    full = lambda shape: pl.BlockSpec(shape, lambda i: (0,) * len(shape))
    return pl.pallas_call(
        _mem_kv_kernel,
        grid=(b,),
        in_specs=[pl.BlockSpec((1, m, d), lambda i: (i, 0, 0)), full((1, d)), full(wk.shape), full(wv.shape)],
        out_specs=[pl.BlockSpec((1, m, d), lambda i: (i, 0, 0))] * 2,
        out_shape=[jax.ShapeDtypeStruct((b, m, d), BF16)] * 2,
        compiler_params=pltpu.CompilerParams(dimension_semantics=("arbitrary",), vmem_limit_bytes=VMEM_LIMIT),
        name="mem_kv",
    )(mem, norm_w, wk, wv)


def _swa_block(q, kk, vv, sink_ref, first_block):
    nq = ATTN_BLOCK
    lane = lax.broadcasted_iota(jnp.int32, kk.shape, 1)
    k_sw = pltpu.roll(kk, ATTN_HEAD_DIM, axis=1)
    v_sw = pltpu.roll(vv, ATTN_HEAD_DIM, axis=1)
    low = lane < ATTN_HEAD_DIM

    qi = lax.broadcasted_iota(jnp.int32, (nq, 2 * nq), 0)
    kj = lax.broadcasted_iota(jnp.int32, (nq, 2 * nq), 1)
    diff = qi + nq - kj
    allowed = (diff >= 0) & (diff < nq) & ((kj >= nq) | jnp.logical_not(first_block))
    allowed = jnp.concatenate([allowed] * ATTN_GROUP, axis=0)

    gw = ATTN_GROUP * ATTN_HEAD_DIM
    head_of_lane = lax.broadcasted_iota(jnp.int32, (nq, gw), 1) // ATTN_HEAD_DIM
    outs = []
    for j in range(ATTN_KV_HEADS):
        kj2 = jnp.where(low, kk, k_sw) if j == 0 else jnp.where(low, k_sw, kk)
        vj2 = jnp.where(low, vv, v_sw) if j == 0 else jnp.where(low, v_sw, vv)
        k_rep = jnp.concatenate([kj2, kj2], axis=1).astype(BF16)
        v_rep = jnp.concatenate([vj2, vj2], axis=1).astype(BF16)
        qg = q[:, j * gw:(j + 1) * gw]
        qs = jnp.concatenate([jnp.where(head_of_lane == h, qg, 0.0) for h in range(ATTN_GROUP)], axis=0).astype(BF16)
        s = _dot_nt(qs, k_rep)
        s = jnp.where(allowed, s, -jnp.inf)
        sink = jnp.concatenate(
            [jnp.full((nq, 1), sink_ref[j * ATTN_GROUP + h], F32) for h in range(ATTN_GROUP)], axis=0)
        m = jnp.maximum(jnp.max(s, axis=-1, keepdims=True), sink)
        p = jnp.exp(s - m)
        denom = jnp.sum(p, axis=-1, keepdims=True) + jnp.exp(sink - m)
        p = (p * (1.0 / denom)).astype(BF16)
        o = _dot(p, v_rep)
        og = jnp.where(head_of_lane == 0, o[0:nq], 0.0)
        for h in range(1, ATTN_GROUP):
            og = og + jnp.where(head_of_lane == h, o[h * nq:(h + 1) * nq], 0.0)
        outs.append(og)
    return jnp.concatenate(outs, axis=1)


def _hgrn_chunk(qr, fr, iv, gr, lb, onw, s_ref):
    c = HGRN_CHUNK
    q = qr * jax.nn.sigmoid(qr) * (HGRN_DIM ** -0.5)
    f = lb + (1.0 - lb) * jax.nn.sigmoid(fr)
    k = (1.0 - lb) * jax.nn.sigmoid(-fr)
    g = jnp.log(f)

    ti = lax.broadcasted_iota(jnp.int32, (c, c), 0)
    si = lax.broadcasted_iota(jnp.int32, (c, c), 1)
    tril = (si <= ti)
    ltri = tril.astype(BF16)
    g_hi = g.astype(BF16)
    g_lo = (g - g_hi.astype(F32)).astype(BF16)
    bc = _dot(ltri, g_hi) + _dot(ltri, g_lo)

    row = lax.broadcasted_iota(jnp.int32, (c, HGRN_WIDTH), 0)
    p_blocks = [[] for _ in range(HGRN_HEADS)]
    for i in range(c // HGRN_SUB):
        r0 = i * HGRN_SUB
        anchor = bc[r0 + HGRN_SUB // 2 - 1:r0 + HGRN_SUB // 2, :]
        qe = (q[r0:r0 + HGRN_SUB] * jnp.exp(bc[r0:r0 + HGRN_SUB] - anchor)).astype(BF16)
        ke = (k * jnp.exp(jnp.where(row < r0 + HGRN_SUB, anchor - bc, -jnp.inf))).astype(BF16)
        for h in range(HGRN_HEADS):
            hs = slice(h * HGRN_DIM, (h + 1) * HGRN_DIM)
            p_blocks[h].append(_dot_nt(qe[:, hs], ke[:, hs]))

    qd = (q * jnp.exp(bc)).astype(BF16)
    b_last = bc[c - 1:c, :]
    kd = (k * jnp.exp(b_last - bc)).astype(BF16)
    s_decay = jnp.exp(b_last)
    v_b = iv.astype(BF16)
    gate = gr * jax.nn.sigmoid(gr)

    outs = []
    for h in range(HGRN_HEADS):
        hs = slice(h * HGRN_DIM, (h + 1) * HGRN_DIM)
        a = jnp.where(tril, jnp.concatenate(p_blocks[h], axis=0), 0.0).astype(BF16)
        st = s_ref[h]
        o = _dot(a, v_b[:, hs]) + _dot_nt(qd[:, hs], st.astype(BF16))
        s_ref[h] = st * s_decay[:, hs] + _dot_tn(v_b[:, hs], kd[:, hs])
        o = o * lax.rsqrt(jnp.mean(o * o, axis=-1, keepdims=True) + EPS) * onw
        outs.append(o * gate[:, hs])
    return jnp.concatenate(outs, axis=1)


def _mixer_kernel(x_ref, prew_ref, win_ref, sink_ref, lbl_ref, onw_ref, wout_ref, postw_ref, o_ref,
                  z_ref, kvp_ref, s_ref, mix_ref):
    n = pl.program_id(1)
    tm = x_ref.shape[1]

    @pl.when(n == 0)
    def _():
        kvp_ref[...] = jnp.zeros_like(kvp_ref)
        s_ref[...] = jnp.zeros_like(s_ref)

    x = x_ref[0]
    h = _rms(x, prew_ref[...]).astype(BF16)
    z_ref[...] = _dot(h, win_ref[...])

    for blk in range(tm // ATTN_BLOCK):
        r0 = blk * ATTN_BLOCK
        q = z_ref[r0:r0 + ATTN_BLOCK, COL_Q:COL_Q + ATTN_WIDTH] * (ATTN_HEAD_DIM ** -0.5)
        k_cur = z_ref[r0:r0 + ATTN_BLOCK, COL_K:COL_K + ATTN_KV_WIDTH]
        v_cur = z_ref[r0:r0 + ATTN_BLOCK, COL_V:COL_V + ATTN_KV_WIDTH]
        if blk == 0:
            k_prev = kvp_ref[:, 0:ATTN_KV_WIDTH]
            v_prev = kvp_ref[:, ATTN_KV_WIDTH:2 * ATTN_KV_WIDTH]
            first_block = n == 0
        else:
            k_prev = z_ref[r0 - ATTN_BLOCK:r0, COL_K:COL_K + ATTN_KV_WIDTH]
            v_prev = z_ref[r0 - ATTN_BLOCK:r0, COL_V:COL_V + ATTN_KV_WIDTH]
            first_block = jnp.bool_(False)
        kk = jnp.concatenate([k_prev, k_cur], axis=0)
        vv = jnp.concatenate([v_prev, v_cur], axis=0)
        mix_ref[r0:r0 + ATTN_BLOCK, 0:ATTN_WIDTH] = _swa_block(q, kk, vv, sink_ref, first_block).astype(BF16)
    kvp_ref[...] = z_ref[tm - ATTN_BLOCK:tm, COL_K:COL_V + ATTN_KV_WIDTH]

    l0 = lbl_ref[0:1, :]
    l1 = lbl_ref[1:2, :]
    lm = jnp.maximum(l0, l1)
    e0 = jnp.exp(l0 - lm)
    lb = e0 / (e0 + jnp.exp(l1 - lm))

    for ch in range(tm // HGRN_CHUNK):
        r0 = ch * HGRN_CHUNK
        rows = slice(r0, r0 + HGRN_CHUNK)
        rec = _hgrn_chunk(z_ref[rows, COL_HQ:COL_HQ + HGRN_WIDTH], z_ref[rows, COL_HF:COL_HF + HGRN_WIDTH],
                          z_ref[rows, COL_HI:COL_HI + HGRN_WIDTH], z_ref[rows, COL_HG:COL_HG + HGRN_WIDTH],
                          lb, onw_ref[...], s_ref)
        mix_ref[rows, ATTN_WIDTH:ATTN_WIDTH + HGRN_WIDTH] = rec.astype(BF16)

    m = _dot(mix_ref[...], wout_ref[...])
    o_ref[0] = x + _rms(m, postw_ref[...])


def _mixer(x, pre_w, w_in, sinks, lb_logits, out_norm_w, w_out, post_w):
    b, t, d = x.shape
    tm = TM_MIX
    assert t % tm == 0 and d == D_MODEL and w_in.shape == (D_MODEL, IN_PROJ_WIDTH)
    full = lambda shape: pl.BlockSpec(shape, lambda i, j: (0,) * len(shape))
    return pl.pallas_call(
        _mixer_kernel,
        grid=(b, t // tm),
        in_specs=[
            pl.BlockSpec((1, tm, d), lambda i, j: (i, j, 0)),
            full((1, d)),
            full(w_in.shape),
            pl.BlockSpec(memory_space=pltpu.SMEM),
            full(lb_logits.shape),
            full((1, HGRN_DIM)),
            full(w_out.shape),
            full((1, d)),
        ],
        out_specs=pl.BlockSpec((1, tm, d), lambda i, j: (i, j, 0)),
        out_shape=jax.ShapeDtypeStruct((b, t, d), F32),
        scratch_shapes=[
            pltpu.VMEM((tm, IN_PROJ_WIDTH), F32),
            pltpu.VMEM((ATTN_BLOCK, 2 * ATTN_KV_WIDTH), F32),
            pltpu.VMEM((HGRN_HEADS, HGRN_DIM, HGRN_DIM), F32),
            pltpu.VMEM((tm, ATTN_WIDTH + HGRN_WIDTH), BF16),
        ],
        compiler_params=pltpu.CompilerParams(
            dimension_semantics=("arbitrary", "arbitrary"), vmem_limit_bytes=VMEM_LIMIT),
        name="mixer",
    )(x, pre_w, w_in, sinks, lb_logits, out_norm_w, w_out, post_w)


def _xattn_kernel(x_ref, prew_ref, wq_ref, k_ref, v_ref, wo_ref, postw_ref, o_ref, cat_ref):
    x = x_ref[0]
    h = _rms(x, prew_ref[...]).astype(BF16)
    q = (_dot(h, wq_ref[...]) * (CA_HEAD_DIM ** -0.5)).astype(BF16)
    def scores(hd):
        hs = slice(hd * CA_HEAD_DIM, (hd + 1) * CA_HEAD_DIM)
        return _dot_nt(q[:, hs], k_ref[0, :, hs])

    s_next = scores(0)
    for hd in range(CA_HEADS):
        hs = slice(hd * CA_HEAD_DIM, (hd + 1) * CA_HEAD_DIM)
        s = s_next
        if hd + 1 < CA_HEADS:
            s_next = scores(hd + 1)
        m = jnp.max(s, axis=-1, keepdims=True)
        p = jnp.exp(s - m)
        p = (p * (1.0 / jnp.sum(p, axis=-1, keepdims=True))).astype(BF16)
        cat_ref[:, hs] = _dot(p, v_ref[0, :, hs]).astype(BF16)
    c = _dot(cat_ref[...], wo_ref[...])
    o_ref[0] = x + _rms(c, postw_ref[...])


def _xattn(x, pre_w, wq, k, v, wo, post_w):
    b, t, d = x.shape
    tm = TM_CA
    nm = k.shape[1]
    assert t % tm == 0
    full = lambda shape: pl.BlockSpec(shape, lambda i, j: (0,) * len(shape))
    return pl.pallas_call(
        _xattn_kernel,
        grid=(b, t // tm),
        in_specs=[
            pl.BlockSpec((1, tm, d), lambda i, j: (i, j, 0)),
            full((1, d)),
            full(wq.shape),
            pl.BlockSpec((1, nm, d), lambda i, j: (i, 0, 0)),
            pl.BlockSpec((1, nm, d), lambda i, j: (i, 0, 0)),
            full(wo.shape),
            full((1, d)),
        ],
        out_specs=pl.BlockSpec((1, tm, d), lambda i, j: (i, j, 0)),
        out_shape=jax.ShapeDtypeStruct((b, t, d), F32),
        scratch_shapes=[pltpu.VMEM((tm, d), BF16)],
        compiler_params=pltpu.CompilerParams(
            dimension_semantics=("arbitrary", "arbitrary"), vmem_limit_bytes=VMEM_LIMIT),
        name="xattn",
    )(x, pre_w, wq, k, v, wo, post_w)


def _gelu_tanh(x):
    return 0.5 * x * (1.0 + jnp.tanh(0.7978845608028654 * (x + 0.044715 * (x * x * x))))


def _causal_conv3(u_ref, kb, tm, cw_ref, cb_ref):
    ls = slice(kb * LANES, (kb + 1) * LANES)
    out = (cw_ref[2:3, ls] * u_ref[kb, HALO:HALO + tm, :] + cw_ref[1:2, ls] * u_ref[kb, HALO - 1:HALO - 1 + tm, :]
           + cw_ref[0:1, ls] * u_ref[kb, HALO - 2:HALO - 2 + tm, :]) + cb_ref[:, ls]
    u_ref[kb, 0:HALO, :] = u_ref[kb, tm:tm + HALO, :]
    return out


def _ffn_kernel(x_ref, prew_ref, wup_ref, cw_ref, cb_ref, wdn_ref, postw_ref, o_ref, u_ref):
    n = pl.program_id(1)
    tm = x_ref.shape[1]
    n_chunks = D_FF // FF_CHUNK
    kb_per_chunk = FF_CHUNK // LANES
    kb_val = D_FF // LANES

    @pl.when(n == 0)
    def _():
        u_ref[:, 0:HALO, :] = jnp.zeros((u_ref.shape[0], HALO, LANES), F32)

    x = x_ref[0]
    h = _rms(x, prew_ref[...]).astype(BF16)

    def up(c):
        for base, col0 in ((0, 0), (kb_val, D_FF)):
            u = _dot(h, wup_ref[:, col0 + c * FF_CHUNK:col0 + (c + 1) * FF_CHUNK])
            for j in range(kb_per_chunk):
                u_ref[base + c * kb_per_chunk + j, HALO:HALO + tm, :] = u[:, j * LANES:(j + 1) * LANES]

    acc = jnp.zeros((tm, D_MODEL), F32)
    up(0)
    for c in range(n_chunks):
        if c + 1 < n_chunks:
            up(c + 1)
        acts = []
        for j in range(kb_per_chunk):
            kb = c * kb_per_chunk + j
            gate = _causal_conv3(u_ref, kb, tm, cw_ref, cb_ref)
            val = _causal_conv3(u_ref, kb_val + kb, tm, cw_ref, cb_ref)
            acts.append((_gelu_tanh(gate) * val).astype(BF16))
        act = jnp.concatenate(acts, axis=1)
        acc = acc + _dot(act, wdn_ref[c * FF_CHUNK:(c + 1) * FF_CHUNK, :])
    o_ref[0] = x + _rms(acc, postw_ref[...])


def _ffn(x, pre_w, w_up, conv_w, conv_b, w_down, post_w):
    b, t, d = x.shape
    tm = TM_FFN
    assert t % tm == 0 and w_up.shape == (D_MODEL, 2 * D_FF) and D_FF % FF_CHUNK == 0
    full = lambda shape: pl.BlockSpec(shape, lambda i, j: (0,) * len(shape))
    return pl.pallas_call(
        _ffn_kernel,
        grid=(b, t // tm),
        in_specs=[
            pl.BlockSpec((1, tm, d), lambda i, j: (i, j, 0)),
            full((1, d)),
            full(w_up.shape),
            full(conv_w.shape),
            full((1, 2 * D_FF)),
            full(w_down.shape),
            full((1, d)),
        ],
        out_specs=pl.BlockSpec((1, tm, d), lambda i, j: (i, j, 0)),
        out_shape=jax.ShapeDtypeStruct((b, t, d), F32),
        scratch_shapes=[pltpu.VMEM((2 * D_FF // LANES, HALO + tm, LANES), F32)],
        compiler_params=pltpu.CompilerParams(
            dimension_semantics=("arbitrary", "arbitrary"), vmem_limit_bytes=VMEM_LIMIT),
        name="ffn",
    )(x, pre_w, w_up, conv_w, conv_b, w_down, post_w)


def kernel(x, mem, mix_pre_norm, w_in, attn_sinks, hgrn_lb_logits, hgrn_out_norm, w_out, mix_post_norm, ca_pre_norm,
           mem_norm, ca_wq, ca_wk, ca_wv, ca_wo, ca_post_norm, ffn_pre_norm, ffn_w_up, ffn_conv_w, ffn_conv_b,
           ffn_w_down, ffn_post_norm):
    depth = w_in.shape[0]
    assert depth == 1 and hgrn_lb_logits.shape[0] == 2
    bf = lambda w: w.astype(BF16)
    for l in range(depth):
        x = _mixer(x, mix_pre_norm[l:l + 1], bf(w_in[l]), attn_sinks[l], hgrn_lb_logits, hgrn_out_norm[l:l + 1],
                   bf(w_out[l]), mix_post_norm[l:l + 1])
        k, v = _mem_kv(mem, mem_norm[l:l + 1], bf(ca_wk[l]), bf(ca_wv[l]))
        x = _xattn(x, ca_pre_norm[l:l + 1], bf(ca_wq[l]), k, v, bf(ca_wo[l]), ca_post_norm[l:l + 1])
        x = _ffn(x, ffn_pre_norm[l:l + 1], bf(ffn_w_up[l]), ffn_conv_w[l], ffn_conv_b[l:l + 1], bf(ffn_w_down[l]),
                 ffn_post_norm[l:l + 1])
    return x
```

```python
import jax
import jax.numpy as jnp
from jax import lax
from jax.experimental import pallas as pl
from jax.experimental.pallas import tpu as pltpu

F32 = jnp.float32
BF16 = jnp.bfloat16

EPS = 1e-6
D_MODEL = 1024
LANES = 128
ATTN_WIDTH = 512
ATTN_HEAD_DIM = 64
ATTN_GROUP = 4
ATTN_KV_HEADS = 2
ATTN_KV_WIDTH = ATTN_KV_HEADS * ATTN_HEAD_DIM
ATTN_BLOCK = 128
HGRN_HEADS = 4
HGRN_DIM = 128
HGRN_WIDTH = HGRN_HEADS * HGRN_DIM
HGRN_PAIR = 2 * HGRN_DIM
COL_Q = 0
COL_K = COL_Q + ATTN_WIDTH
COL_V = COL_K + ATTN_KV_WIDTH
COL_HQ = COL_V + ATTN_KV_WIDTH
COL_HF = COL_HQ + HGRN_WIDTH
COL_HI = COL_HF + HGRN_WIDTH
COL_HG = COL_HI + HGRN_WIDTH
IN_PROJ_WIDTH = COL_HG + HGRN_WIDTH
HGRN_CHUNK = 64
HGRN_SUB = 16
HGRN_NSUB = HGRN_CHUNK // HGRN_SUB
CA_HEADS = 4
CA_HEAD_DIM = 256
D_FF = 2816
FF_CHUNK = 256
FF_DOWN_GROUP = 4
HALO = 8

TM_MIX = 512
TM_CA = 512
TM_FFN = 512

VMEM_LIMIT = 56 * 1024 * 1024

NT_DIMS = (((1,), (1,)), ((), ()))
TN_DIMS = (((0,), (0,)), ((), ()))


def _rms(x, w):
    return x * lax.rsqrt(jnp.mean(x * x, axis=-1, keepdims=True) + EPS) * w


def _dot(a, b):
    return jnp.dot(a, b, preferred_element_type=F32)


def _dot_nt(a, b):
    return lax.dot_general(a, b, NT_DIMS, preferred_element_type=F32)


def _dot_tn(a, b):
    return lax.dot_general(a, b, TN_DIMS, preferred_element_type=F32)


def _mem_kv_kernel(mem_ref, nw_ref, wk_ref, wv_ref, k_ref, v_ref):
    mn = _rms(mem_ref[0], nw_ref[...]).astype(BF16)
    k_ref[0] = _dot(mn, wk_ref[...]).astype(BF16)
    v_ref[0] = _dot(mn, wv_ref[...]).astype(BF16)


def _mem_kv(mem, norm_w, wk, wv):
    b, m, d = mem.shape
    full = lambda shape: pl.BlockSpec(shape, lambda i: (0,) * len(shape))
    return pl.pallas_call(
        _mem_kv_kernel,
        grid=(b,),
        in_specs=[pl.BlockSpec((1, m, d), lambda i: (i, 0, 0)), full((1, d)), full(wk.shape), full(wv.shape)],
        out_specs=[pl.BlockSpec((1, m, d), lambda i: (i, 0, 0))] * 2,
        out_shape=[jax.ShapeDtypeStruct((b, m, d), BF16)] * 2,
        compiler_params=pltpu.CompilerParams(dimension_semantics=("arbitrary",), vmem_limit_bytes=VMEM_LIMIT),
        name="mem_kv",
    )(mem, norm_w, wk, wv)


def _swa_scores(z_ref, kv_ref, r_ref, row0, blk, j):
    nq = ATTN_BLOCK
    gw = ATTN_GROUP * ATTN_HEAD_DIM
    r0 = blk * nq
    q = z_ref[r0:r0 + nq, COL_Q + j * gw:COL_Q + (j + 1) * gw] * (ATTN_HEAD_DIM ** -0.5)
    kk = kv_ref[r0:r0 + 2 * nq, 0:ATTN_KV_WIDTH]
    k_sw = pltpu.roll(kk, ATTN_HEAD_DIM, axis=1)
    low = lax.broadcasted_iota(jnp.int32, kk.shape, 1) < ATTN_HEAD_DIM
    kj = jnp.where(low, kk, k_sw) if j == 0 else jnp.where(low, k_sw, kk)
    k_rep = jnp.concatenate([kj, kj], axis=1).astype(BF16)
    head_of_lane = lax.broadcasted_iota(jnp.int32, (nq, gw), 1) // ATTN_HEAD_DIM
    qs = jnp.concatenate([jnp.where(head_of_lane == h, q, 0.0) for h in range(ATTN_GROUP)], axis=0).astype(BF16)
    r_ref[row0:row0 + ATTN_GROUP * nq, :] = _dot_nt(qs, k_rep)


def _swa_finish(r_ref, s_row0, o_row0, kv_ref, sink_ref, mix_ref, blk, j, first_block):
    nq = ATTN_BLOCK
    s = r_ref[s_row0:s_row0 + ATTN_GROUP * nq, :]
    gw = ATTN_GROUP * ATTN_HEAD_DIM
    r0 = blk * nq
    qi = lax.broadcasted_iota(jnp.int32, (nq, 2 * nq), 0)
    kj = lax.broadcasted_iota(jnp.int32, (nq, 2 * nq), 1)
    diff = qi + nq - kj
    allowed = (diff >= 0) & (diff < nq) & ((kj >= nq) | jnp.logical_not(first_block))
    ps, denoms = [], []
    for h in range(ATTN_GROUP):
        sink = sink_ref[j * ATTN_GROUP + h]
        sh = jnp.where(allowed, s[h * nq:(h + 1) * nq], -jnp.inf)
        mh = jnp.maximum(jnp.max(sh, axis=-1, keepdims=True), sink)
        ph = jnp.exp(sh - mh)
        ps.append(ph.astype(BF16))
        denoms.append(jnp.sum(ph, axis=-1, keepdims=True) + jnp.exp(sink - mh))
    p = jnp.concatenate(ps, axis=0)

    vv = kv_ref[r0:r0 + 2 * nq, ATTN_KV_WIDTH:2 * ATTN_KV_WIDTH]
    v_sw = pltpu.roll(vv, ATTN_HEAD_DIM, axis=1)
    low = lax.broadcasted_iota(jnp.int32, vv.shape, 1) < ATTN_HEAD_DIM
    vj = jnp.where(low, vv, v_sw) if j == 0 else jnp.where(low, v_sw, vv)
    v_rep = jnp.concatenate([vj, vj], axis=1).astype(BF16)
    r_ref[o_row0:o_row0 + ATTN_GROUP * nq, :] = _dot(p, v_rep)
    head_of_lane = lax.broadcasted_iota(jnp.int32, (nq, gw), 1) // ATTN_HEAD_DIM
    og = None
    for h in range(ATTN_GROUP):
        oh = r_ref[o_row0 + h * nq:o_row0 + (h + 1) * nq, :] * (1.0 / denoms[h])
        oh = jnp.where(head_of_lane == h, oh, 0.0)
        og = oh if og is None else og + oh
    mix_ref[r0:r0 + nq, j * gw:(j + 1) * gw] = og.astype(BF16)


def _hgrn_gates(z_ref, lb, q_ref, k_ref, ghl_ref, rows):
    qr = z_ref[rows, COL_HQ:COL_HQ + HGRN_WIDTH]
    fr = z_ref[rows, COL_HF:COL_HF + HGRN_WIDTH]
    q_ref[rows, :] = qr * jax.nn.sigmoid(qr) * (HGRN_DIM ** -0.5)
    f = lb + (1.0 - lb) * jax.nn.sigmoid(fr)
    k_ref[rows, :] = (1.0 - lb) * jax.nn.sigmoid(-fr)
    g = jnp.log(f)
    g_hi = g.astype(BF16)
    ghl_ref[rows, 0:HGRN_WIDTH] = g_hi
    ghl_ref[rows, HGRN_WIDTH:2 * HGRN_WIDTH] = (g - g_hi.astype(F32)).astype(BF16)


def _hgrn_cumsum(ghl_ref, cs_ref, bc_ref):
    tm = bc_ref.shape[0]
    ti = lax.broadcasted_iota(jnp.int32, (tm, tm), 0)
    si = lax.broadcasted_iota(jnp.int32, (tm, tm), 1)
    same_chunk = jnp.bitwise_xor(ti, si) < HGRN_CHUNK
    tri = ((si <= ti) & same_chunk).astype(BF16)
    cs_ref[...] = _dot(tri, ghl_ref[...])
    bc_ref[...] = cs_ref[:, 0:HGRN_WIDTH] + cs_ref[:, HGRN_WIDTH:2 * HGRN_WIDTH]


def _hgrn_factors(c, q_ref, k_ref, bc_ref, qe_ref, kecat_ref, qd_ref, kd_ref):
    r0c = c * HGRN_CHUNK
    rows = slice(r0c, r0c + HGRN_CHUNK)
    bc = bc_ref[rows, :]
    q = q_ref[rows, :]
    k = k_ref[rows, :]
    row = lax.broadcasted_iota(jnp.int32, (HGRN_CHUNK, HGRN_WIDTH), 0)
    for i in range(HGRN_NSUB):
        r0 = i * HGRN_SUB
        anchor = bc[r0 + HGRN_SUB // 2 - 1:r0 + HGRN_SUB // 2, :]
        qe_ref[r0c + r0:r0c + r0 + HGRN_SUB, :] = (q[r0:r0 + HGRN_SUB] * jnp.exp(bc[r0:r0 + HGRN_SUB] - anchor)
                                                   ).astype(BF16)
        ke = k * jnp.exp(jnp.where(row < r0 + HGRN_SUB, anchor - bc, -jnp.inf))
        kecat_ref[c, i * HGRN_CHUNK:(i + 1) * HGRN_CHUNK, :] = ke.astype(BF16)
    qd_ref[rows, :] = (q * jnp.exp(bc)).astype(BF16)
    b_last = bc[HGRN_CHUNK - 1:HGRN_CHUNK, :]
    e = c % 2
    kd_ref[c, e * HGRN_CHUNK:(e + 1) * HGRN_CHUNK, :] = (k * jnp.exp(b_last - bc)).astype(BF16)
    kd_ref[c, (1 - e) * HGRN_CHUNK:(2 - e) * HGRN_CHUNK, :] = jnp.zeros((HGRN_CHUNK, HGRN_WIDTH), BF16)
    return jnp.exp(b_last)


def _hgrn_scores(c, qe_ref, kecat_ref, r_ref, row0):
    rows = slice(c * HGRN_CHUNK, (c + 1) * HGRN_CHUNK)
    for h in range(HGRN_HEADS):
        hs = slice(h * HGRN_DIM, (h + 1) * HGRN_DIM)
        r_ref[row0 + h * HGRN_CHUNK:row0 + (h + 1) * HGRN_CHUNK, :] = _dot_nt(qe_ref[rows, hs], kecat_ref[c, :, hs])


def _hgrn_scores_finish(c, r_ref, row0, a_ref):
    rows = slice(c * HGRN_CHUNK, (c + 1) * HGRN_CHUNK)
    t = lax.broadcasted_iota(jnp.int32, (HGRN_CHUNK, LANES), 0)
    l = lax.broadcasted_iota(jnp.int32, (HGRN_CHUNK, LANES), 1)
    keep = ((l // HGRN_CHUNK) == ((t // HGRN_SUB) % 2)) & ((l % HGRN_CHUNK) <= t)
    half = HGRN_CHUNK // 2
    for h in range(HGRN_HEADS):
        hs = slice(h * HGRN_DIM, (h + 1) * HGRN_DIM)
        pr = row0 + h * HGRN_CHUNK
        a = jnp.concatenate([r_ref[pr:pr + half, 0:LANES], r_ref[pr + half:pr + HGRN_CHUNK, LANES:2 * LANES]], axis=0)
        a_ref[rows, hs] = jnp.where(keep, a, 0.0).astype(BF16)


def _pair_diag_mask():
    r = lax.broadcasted_iota(jnp.int32, (HGRN_PAIR, HGRN_PAIR), 0)
    l = lax.broadcasted_iota(jnp.int32, (HGRN_PAIR, HGRN_PAIR), 1)
    return (r // HGRN_DIM) == (l // HGRN_DIM)


def _hgrn_transpose_v(z_ref, vt_ref):
    eye = (lax.broadcasted_iota(jnp.int32, (LANES, LANES), 0)
           == lax.broadcasted_iota(jnp.int32, (LANES, LANES), 1)).astype(BF16)
    for a in range(HGRN_WIDTH // LANES):
        v = z_ref[:, COL_HI + a * LANES:COL_HI + (a + 1) * LANES].astype(BF16)
        vt_ref[a * LANES:(a + 1) * LANES, :] = _dot_nt(eye, v).astype(BF16)


def _hgrn_states(n_chunks, vt_ref, kd_ref, st_ref, stbf_ref, decays, r_ref, row0):
    diag = _pair_diag_mask()
    for p in range(HGRN_HEADS // 2):
        ps = slice(p * HGRN_PAIR, (p + 1) * HGRN_PAIR)
        for c in range(n_chunks):
            tok = slice((c // 2) * LANES, (c // 2 + 1) * LANES)
            ur = row0 + (p * n_chunks + c) * HGRN_PAIR
            r_ref[ur:ur + HGRN_PAIR, :] = _dot(vt_ref[ps, tok], kd_ref[c, :, ps])
        st = st_ref[p]
        for c in range(n_chunks):
            ur = row0 + (p * n_chunks + c) * HGRN_PAIR
            stbf_ref[c, p] = st.astype(BF16)
            st = st * decays[c][:, ps] + jnp.where(diag, r_ref[ur:ur + HGRN_PAIR, :], 0.0)
        st_ref[p] = st


def _hgrn_outputs(c, z_ref, a_ref, qd_ref, stbf_ref, onw, mix_ref, r_ref, row0):
    rows = slice(c * HGRN_CHUNK, (c + 1) * HGRN_CHUNK)
    diag = _pair_diag_mask()
    for p in range(HGRN_HEADS // 2):
        ps = slice(p * HGRN_PAIR, (p + 1) * HGRN_PAIR)
        vp = z_ref[rows, COL_HI + p * HGRN_PAIR:COL_HI + (p + 1) * HGRN_PAIR].astype(BF16)
        v_bd = jnp.where(diag, jnp.concatenate([vp] * (HGRN_PAIR // HGRN_CHUNK), axis=0), 0.0)
        orow = row0 + p * HGRN_CHUNK
        r_ref[orow:orow + HGRN_CHUNK, :] = _dot(a_ref[rows, ps], v_bd) + _dot_nt(qd_ref[rows, ps], stbf_ref[c, p])
        o = r_ref[orow:orow + HGRN_CHUNK, :]
        gr = z_ref[rows, COL_HG + p * HGRN_PAIR:COL_HG + (p + 1) * HGRN_PAIR]
        gate = gr * jax.nn.sigmoid(gr)
        outs = []
        for hh in range(2):
            oh = o[:, hh * HGRN_DIM:(hh + 1) * HGRN_DIM]
            outs.append(oh * lax.rsqrt(jnp.mean(oh * oh, axis=-1, keepdims=True) + EPS) * onw)
        rec = jnp.concatenate(outs, axis=1) * gate
        mix_ref[rows, ATTN_WIDTH + p * HGRN_PAIR:ATTN_WIDTH + (p + 1) * HGRN_PAIR] = rec.astype(BF16)


def _mixer_kernel(x_ref, prew_ref, win_ref, sink_ref, lbl_ref, onw_ref, wout_ref, postw_ref, o_ref,
                  z_ref, kv_ref, st_ref, mix_ref, q_ref, k_ref, ghl_ref, bc_ref, qe_ref, kecat_ref, qd_ref, kd_ref,
                  a_ref, vt_ref, stbf_ref, cs_ref, r_ref):
    n = pl.program_id(1)
    tm = x_ref.shape[1]
    n_chunks = tm // HGRN_CHUNK
    units = [(blk, j) for blk in range(tm // ATTN_BLOCK) for j in range(ATTN_KV_HEADS)]

    @pl.when(n == 0)
    def _():
        kv_ref[0:ATTN_BLOCK, :] = jnp.zeros((ATTN_BLOCK, 2 * ATTN_KV_WIDTH), F32)
        st_ref[...] = jnp.zeros_like(st_ref)

    x = x_ref[0]
    h = _rms(x, prew_ref[...]).astype(BF16)
    z_ref[:, COL_HQ:IN_PROJ_WIDTH] = _dot(h, win_ref[:, COL_HQ:IN_PROJ_WIDTH])
    z_ref[:, 0:COL_HQ] = _dot(h, win_ref[:, 0:COL_HQ])
    kv_ref[ATTN_BLOCK:ATTN_BLOCK + tm, :] = z_ref[:, COL_K:COL_HQ]

    l0 = lbl_ref[0:1, :]
    l1 = lbl_ref[1:2, :]
    lm = jnp.maximum(l0, l1)
    e0 = jnp.exp(l0 - lm)
    lb = e0 / (e0 + jnp.exp(l1 - lm))

    for c in range(n_chunks):
        _hgrn_gates(z_ref, lb, q_ref, k_ref, ghl_ref, slice(c * HGRN_CHUNK, (c + 1) * HGRN_CHUNK))
    _hgrn_transpose_v(z_ref, vt_ref)
    _hgrn_cumsum(ghl_ref, cs_ref, bc_ref)

    def first_block(blk):
        return (n == 0) if blk == 0 else jnp.bool_(False)

    unit_rows = ATTN_GROUP * ATTN_BLOCK
    s_base = 0
    pv_base = s_base + len(units) * unit_rows
    p_base = pv_base + len(units) * unit_rows
    upd_base = p_base + n_chunks * HGRN_HEADS * HGRN_CHUNK
    o_base = upd_base + n_chunks * HGRN_HEADS // 2 * HGRN_PAIR
    assert o_base + n_chunks * HGRN_HEADS // 2 * HGRN_CHUNK == r_ref.shape[0]

    def p_rows(c):
        return p_base + c * HGRN_HEADS * HGRN_CHUNK

    def o_rows(c):
        return o_base + c * HGRN_HEADS // 2 * HGRN_CHUNK

    _swa_scores(z_ref, kv_ref, r_ref, s_base, *units[0])
    decays = []
    for c in range(n_chunks):
        decays.append(_hgrn_factors(c, q_ref, k_ref, bc_ref, qe_ref, kecat_ref, qd_ref, kd_ref))
        _hgrn_scores(c, qe_ref, kecat_ref, r_ref, p_rows(c))
    _hgrn_states(n_chunks, vt_ref, kd_ref, st_ref, stbf_ref, decays, r_ref, upd_base)
    for c in range(n_chunks):
        _hgrn_scores_finish(c, r_ref, p_rows(c), a_ref)
    for u, (blk, j) in enumerate(units):
        if u + 1 < len(units):
            _swa_scores(z_ref, kv_ref, r_ref, s_base + (u + 1) * unit_rows, *units[u + 1])
        if u < n_chunks:
            _hgrn_outputs(u, z_ref, a_ref, qd_ref, stbf_ref, onw_ref[...], mix_ref, r_ref, o_rows(u))
        _swa_finish(r_ref, s_base + u * unit_rows, pv_base + u * unit_rows, kv_ref, sink_ref, mix_ref, blk, j,
                    first_block(blk))
    for c in range(len(units), n_chunks):
        _hgrn_outputs(c, z_ref, a_ref, qd_ref, stbf_ref, onw_ref[...], mix_ref, r_ref, o_rows(c))
    kv_ref[0:ATTN_BLOCK, :] = kv_ref[tm:tm + ATTN_BLOCK, :]

    m = _dot(mix_ref[...], wout_ref[...])
    o_ref[0] = x + _rms(m, postw_ref[...])


def _mixer_result_rows(tm):
    units = (tm // ATTN_BLOCK) * ATTN_KV_HEADS
    chunks = tm // HGRN_CHUNK
    return (2 * units * ATTN_GROUP * ATTN_BLOCK + chunks * HGRN_HEADS * HGRN_CHUNK
            + chunks * (HGRN_HEADS // 2) * (HGRN_PAIR + HGRN_CHUNK))


def _mixer(x, pre_w, w_in, sinks, lb_logits, out_norm_w, w_out, post_w):
    b, t, d = x.shape
    tm = TM_MIX
    assert t % tm == 0 and d == D_MODEL and w_in.shape == (D_MODEL, IN_PROJ_WIDTH)
    assert tm % ATTN_BLOCK == 0 and tm % HGRN_CHUNK == 0
    full = lambda shape: pl.BlockSpec(shape, lambda i, j: (0,) * len(shape), pipeline_mode=pl.Buffered(1))
    return pl.pallas_call(
        _mixer_kernel,
        grid=(b, t // tm),
        in_specs=[
            pl.BlockSpec((1, tm, d), lambda i, j: (i, j, 0)),
            full((1, d)),
            full(w_in.shape),
            pl.BlockSpec(memory_space=pltpu.SMEM),
            full(lb_logits.shape),
            full((1, HGRN_DIM)),
            full(w_out.shape),
            full((1, d)),
        ],
        out_specs=pl.BlockSpec((1, tm, d), lambda i, j: (i, j, 0)),
        out_shape=jax.ShapeDtypeStruct((b, t, d), F32),
        scratch_shapes=[
            pltpu.VMEM((tm, IN_PROJ_WIDTH), F32),
            pltpu.VMEM((ATTN_BLOCK + tm, 2 * ATTN_KV_WIDTH), F32),
            pltpu.VMEM((HGRN_HEADS // 2, HGRN_PAIR, HGRN_PAIR), F32),
            pltpu.VMEM((tm, ATTN_WIDTH + HGRN_WIDTH), BF16),
            pltpu.VMEM((tm, HGRN_WIDTH), F32),
            pltpu.VMEM((tm, HGRN_WIDTH), F32),
            pltpu.VMEM((tm, 2 * HGRN_WIDTH), BF16),
            pltpu.VMEM((tm, HGRN_WIDTH), F32),
            pltpu.VMEM((tm, HGRN_WIDTH), BF16),
            pltpu.VMEM((tm // HGRN_CHUNK, HGRN_NSUB * HGRN_CHUNK, HGRN_WIDTH), BF16),
            pltpu.VMEM((tm, HGRN_WIDTH), BF16),
            pltpu.VMEM((tm // HGRN_CHUNK, 2 * HGRN_CHUNK, HGRN_WIDTH), BF16),
            pltpu.VMEM((tm, HGRN_WIDTH), BF16),
            pltpu.VMEM((HGRN_WIDTH, tm), BF16),
            pltpu.VMEM((tm // HGRN_CHUNK, HGRN_HEADS // 2, HGRN_PAIR, HGRN_PAIR), BF16),
            pltpu.VMEM((tm, 2 * HGRN_WIDTH), F32),
            pltpu.VMEM((_mixer_result_rows(tm), HGRN_PAIR), F32),
        ],
        compiler_params=pltpu.CompilerParams(
            dimension_semantics=("arbitrary", "arbitrary"), vmem_limit_bytes=VMEM_LIMIT),
        name="mixer",
    )(x, pre_w, w_in, sinks, lb_logits, out_norm_w, w_out, post_w)


def _xattn_kernel(x_ref, prew_ref, wq_ref, k_ref, v_ref, wo_ref, postw_ref, o_ref, cat_ref):
    x = x_ref[0]
    h = _rms(x, prew_ref[...]).astype(BF16)
    q = (_dot(h, wq_ref[...]) * (CA_HEAD_DIM ** -0.5)).astype(BF16)

    def scores(hd):
        hs = slice(hd * CA_HEAD_DIM, (hd + 1) * CA_HEAD_DIM)
        return _dot_nt(q[:, hs], k_ref[0, :, hs])

    s_next = scores(0)
    for hd in range(CA_HEADS):
        hs = slice(hd * CA_HEAD_DIM, (hd + 1) * CA_HEAD_DIM)
        s = s_next
        if hd + 1 < CA_HEADS:
            s_next = scores(hd + 1)
        m = jnp.max(s, axis=-1, keepdims=True)
        p = jnp.exp(s - m)
        p = (p * (1.0 / jnp.sum(p, axis=-1, keepdims=True))).astype(BF16)
        cat_ref[:, hs] = _dot(p, v_ref[0, :, hs]).astype(BF16)
    c = _dot(cat_ref[...], wo_ref[...])
    o_ref[0] = x + _rms(c, postw_ref[...])


def _xattn(x, pre_w, wq, k, v, wo, post_w):
    b, t, d = x.shape
    tm = TM_CA
    nm = k.shape[1]
    assert t % tm == 0
    full = lambda shape: pl.BlockSpec(shape, lambda i, j: (0,) * len(shape), pipeline_mode=pl.Buffered(1))
    return pl.pallas_call(
        _xattn_kernel,
        grid=(b, t // tm),
        in_specs=[
            pl.BlockSpec((1, tm, d), lambda i, j: (i, j, 0)),
            full((1, d)),
            full(wq.shape),
            pl.BlockSpec((1, nm, d), lambda i, j: (i, 0, 0)),
            pl.BlockSpec((1, nm, d), lambda i, j: (i, 0, 0)),
            full(wo.shape),
            full((1, d)),
        ],
        out_specs=pl.BlockSpec((1, tm, d), lambda i, j: (i, j, 0)),
        out_shape=jax.ShapeDtypeStruct((b, t, d), F32),
        scratch_shapes=[pltpu.VMEM((tm, d), BF16)],
        compiler_params=pltpu.CompilerParams(
            dimension_semantics=("arbitrary", "arbitrary"), vmem_limit_bytes=VMEM_LIMIT),
        name="xattn",
    )(x, pre_w, wq, k, v, wo, post_w)


def _gelu_tanh(x):
    return 0.5 * x * (1.0 + jnp.tanh(0.7978845608028654 * (x + 0.044715 * (x * x * x))))


def _causal_conv3(u_ref, kb, tm, cw_ref, cb_ref):
    ls = slice(kb * LANES, (kb + 1) * LANES)
    out = (cw_ref[2:3, ls] * u_ref[kb, HALO:HALO + tm, :] + cw_ref[1:2, ls] * u_ref[kb, HALO - 1:HALO - 1 + tm, :]
           + cw_ref[0:1, ls] * u_ref[kb, HALO - 2:HALO - 2 + tm, :]) + cb_ref[:, ls]
    u_ref[kb, 0:HALO, :] = u_ref[kb, tm:tm + HALO, :]
    return out


def _ffn_kernel(x_ref, prew_ref, wup_ref, cw_ref, cb_ref, wdn_ref, postw_ref, o_ref, u_ref, act_ref):
    n = pl.program_id(1)
    tm = x_ref.shape[1]
    n_chunks = D_FF // FF_CHUNK
    kb_per_chunk = FF_CHUNK // LANES
    kb_val = D_FF // LANES

    @pl.when(n == 0)
    def _():
        u_ref[:, 0:HALO, :] = jnp.zeros((u_ref.shape[0], HALO, LANES), F32)

    x = x_ref[0]
    h = _rms(x, prew_ref[...]).astype(BF16)

    def up(c):
        for base, col0 in ((0, 0), (kb_val, D_FF)):
            u = _dot(h, wup_ref[:, col0 + c * FF_CHUNK:col0 + (c + 1) * FF_CHUNK])
            for j in range(kb_per_chunk):
                u_ref[base + c * kb_per_chunk + j, HALO:HALO + tm, :] = u[:, j * LANES:(j + 1) * LANES]

    acc = None
    group_start = 0
    up(0)
    for c in range(n_chunks):
        if c + 1 < n_chunks:
            up(c + 1)
        for j in range(kb_per_chunk):
            kb = c * kb_per_chunk + j
            gate = _causal_conv3(u_ref, kb, tm, cw_ref, cb_ref)
            val = _causal_conv3(u_ref, kb_val + kb, tm, cw_ref, cb_ref)
            act_ref[:, kb * LANES:(kb + 1) * LANES] = (_gelu_tanh(gate) * val).astype(BF16)
        if (c + 1) % FF_DOWN_GROUP == 0 or c + 1 == n_chunks:
            ks = slice(group_start * FF_CHUNK, (c + 1) * FF_CHUNK)
            part = _dot(act_ref[:, ks], wdn_ref[ks, :])
            acc = part if acc is None else acc + part
            group_start = c + 1
    o_ref[0] = x + _rms(acc, postw_ref[...])


def _ffn(x, pre_w, w_up, conv_w, conv_b, w_down, post_w):
    b, t, d = x.shape
    tm = TM_FFN
    assert t % tm == 0 and w_up.shape == (D_MODEL, 2 * D_FF) and D_FF % FF_CHUNK == 0
    full = lambda shape: pl.BlockSpec(shape, lambda i, j: (0,) * len(shape), pipeline_mode=pl.Buffered(1))
    return pl.pallas_call(
        _ffn_kernel,
        grid=(b, t // tm),
        in_specs=[
            pl.BlockSpec((1, tm, d), lambda i, j: (i, j, 0)),
            full((1, d)),
            full(w_up.shape),
            full(conv_w.shape),
            full((1, 2 * D_FF)),
            full(w_down.shape),
            full((1, d)),
        ],
        out_specs=pl.BlockSpec((1, tm, d), lambda i, j: (i, j, 0)),
        out_shape=jax.ShapeDtypeStruct((b, t, d), F32),
        scratch_shapes=[pltpu.VMEM((2 * D_FF // LANES, HALO + tm, LANES), F32), pltpu.VMEM((tm, D_FF), BF16)],
        compiler_params=pltpu.CompilerParams(
            dimension_semantics=("arbitrary", "arbitrary"), vmem_limit_bytes=VMEM_LIMIT),
        name="ffn",
    )(x, pre_w, w_up, conv_w, conv_b, w_down, post_w)


def kernel(x, mem, mix_pre_norm, w_in, attn_sinks, hgrn_lb_logits, hgrn_out_norm, w_out, mix_post_norm, ca_pre_norm,
           mem_norm, ca_wq, ca_wk, ca_wv, ca_wo, ca_post_norm, ffn_pre_norm, ffn_w_up, ffn_conv_w, ffn_conv_b,
           ffn_w_down, ffn_post_norm):
    depth = w_in.shape[0]
    assert depth == 1 and hgrn_lb_logits.shape[0] == 2
    bf = lambda w: w.astype(BF16)
    for l in range(depth):
        x = _mixer(x, mix_pre_norm[l:l + 1], bf(w_in[l]), attn_sinks[l], hgrn_lb_logits, hgrn_out_norm[l:l + 1],
                   bf(w_out[l]), mix_post_norm[l:l + 1])
        k, v = _mem_kv(mem, mem_norm[l:l + 1], bf(ca_wk[l]), bf(ca_wv[l]))
        x = _xattn(x, ca_pre_norm[l:l + 1], bf(ca_wq[l]), k, v, bf(ca_wo[l]), ca_post_norm[l:l + 1])
        x = _ffn(x, ffn_pre_norm[l:l + 1], bf(ffn_w_up[l]), ffn_conv_w[l], ffn_conv_b[l:l + 1], bf(ffn_w_down[l]),
                 ffn_post_norm[l:l + 1])
    return x
```

```python
import jax
import jax.numpy as jnp
from jax import lax
from jax.experimental import pallas as pl
from jax.experimental.pallas import tpu as pltpu

F32 = jnp.float32
BF16 = jnp.bfloat16

EPS = 1e-6
LOG2E = 1.4426950408889634
D_MODEL = 1024
LANES = 128
ATTN_WIDTH = 512
ATTN_HEAD_DIM = 64
ATTN_GROUP = 4
ATTN_KV_HEADS = 2
ATTN_KV_WIDTH = ATTN_KV_HEADS * ATTN_HEAD_DIM
ATTN_BLOCK = 128
HGRN_HEADS = 4
HGRN_DIM = 128
HGRN_WIDTH = HGRN_HEADS * HGRN_DIM
HGRN_PAIR = 2 * HGRN_DIM
COL_Q = 0
COL_K = COL_Q + ATTN_WIDTH
COL_V = COL_K + ATTN_KV_WIDTH
COL_HQ = COL_V + ATTN_KV_WIDTH
COL_HF = COL_HQ + HGRN_WIDTH
COL_HI = COL_HF + HGRN_WIDTH
COL_HG = COL_HI + HGRN_WIDTH
IN_PROJ_WIDTH = COL_HG + HGRN_WIDTH
IN_PROJ_STEP = 512
OUT_PROJ_STEP = 512
HGRN_CHUNK = 64
HGRN_SUB = 16
HGRN_NSUB = HGRN_CHUNK // HGRN_SUB
CA_HEADS = 4
CA_HEAD_DIM = 256
D_FF = 2816
FF_CHUNK = 256
FF_DOWN_GROUP = 4
HALO = 8

TM_MIX = 512
TS_MIX = 256
TM_CA = 1024
TM_FFN = 512

VMEM_LIMIT = 56 * 1024 * 1024

NT_DIMS = (((1,), (1,)), ((), ()))
TN_DIMS = (((0,), (0,)), ((), ()))


def _rms(x, w):
    return x * lax.rsqrt(jnp.mean(x * x, axis=-1, keepdims=True) + EPS) * w


def _dot(a, b):
    return jnp.dot(a, b, preferred_element_type=F32)


def _dot_nt(a, b):
    return lax.dot_general(a, b, NT_DIMS, preferred_element_type=F32)


def _dot_tn(a, b):
    return lax.dot_general(a, b, TN_DIMS, preferred_element_type=F32)


def _mem_kv_kernel(mem_ref, nw_ref, wk_ref, wv_ref, k_ref, v_ref):
    mn = _rms(mem_ref[0], nw_ref[...]).astype(BF16)
    k_ref[0] = _dot(mn, wk_ref[...]).astype(BF16)
    v_ref[0] = _dot(mn, wv_ref[...]).astype(BF16)


def _mem_kv(mem, norm_w, wk, wv):
    b, m, d = mem.shape
    full = lambda shape: pl.BlockSpec(shape, lambda i: (0,) * len(shape))
    return pl.pallas_call(
        _mem_kv_kernel,
        grid=(b,),
        in_specs=[pl.BlockSpec((1, m, d), lambda i: (i, 0, 0)), full((1, d)), full(wk.shape), full(wv.shape)],
        out_specs=[pl.BlockSpec((1, m, d), lambda i: (i, 0, 0))] * 2,
        out_shape=[jax.ShapeDtypeStruct((b, m, d), BF16)] * 2,
        compiler_params=pltpu.CompilerParams(dimension_semantics=("arbitrary",), vmem_limit_bytes=VMEM_LIMIT),
        name="mem_kv",
    )(mem, norm_w, wk, wv)


def _swa_scores(z_ref, kv_ref, r_ref, row0, blk, j):
    nq = ATTN_BLOCK
    gw = ATTN_GROUP * ATTN_HEAD_DIM
    r0 = blk * nq
    q = z_ref[r0:r0 + nq, COL_Q + j * gw:COL_Q + (j + 1) * gw] * (ATTN_HEAD_DIM ** -0.5 * LOG2E)
    kk = kv_ref[r0:r0 + 2 * nq, 0:ATTN_KV_WIDTH]
    k_sw = pltpu.roll(kk, ATTN_HEAD_DIM, axis=1)
    low = lax.broadcasted_iota(jnp.int32, kk.shape, 1) < ATTN_HEAD_DIM
    kj = jnp.where(low, kk, k_sw) if j == 0 else jnp.where(low, k_sw, kk)
    k_rep = jnp.concatenate([kj, kj], axis=1).astype(BF16)
    head_of_lane = lax.broadcasted_iota(jnp.int32, (nq, gw), 1) // ATTN_HEAD_DIM
    qs = jnp.concatenate([jnp.where(head_of_lane == h, q, 0.0) for h in range(ATTN_GROUP)], axis=0).astype(BF16)
    r_ref[row0:row0 + ATTN_GROUP * nq, :] = _dot_nt(qs, k_rep)


def _swa_finish(r_ref, s_row0, o_row0, kv_ref, sink_ref, mix_ref, blk, j, first_block):
    nq = ATTN_BLOCK
    s = r_ref[s_row0:s_row0 + ATTN_GROUP * nq, :]
    gw = ATTN_GROUP * ATTN_HEAD_DIM
    r0 = blk * nq
    qi = lax.broadcasted_iota(jnp.int32, (nq, 2 * nq), 0)
    kj = lax.broadcasted_iota(jnp.int32, (nq, 2 * nq), 1)
    diff = qi + nq - kj
    allowed = (diff >= 0) & (diff < nq) & ((kj >= nq) | jnp.logical_not(first_block))
    ps, denoms = [], []
    for h in range(ATTN_GROUP):
        sink = sink_ref[j * ATTN_GROUP + h] * LOG2E
        sh = jnp.where(allowed, s[h * nq:(h + 1) * nq], -jnp.inf)
        mh = jnp.maximum(jnp.max(sh, axis=-1, keepdims=True), sink)
        ph = jnp.exp2(sh - mh)
        ps.append(ph.astype(BF16))
        denoms.append(jnp.sum(ph, axis=-1, keepdims=True) + jnp.exp2(sink - mh))
    p = jnp.concatenate(ps, axis=0)

    vv = kv_ref[r0:r0 + 2 * nq, ATTN_KV_WIDTH:2 * ATTN_KV_WIDTH]
    v_sw = pltpu.roll(vv, ATTN_HEAD_DIM, axis=1)
    low = lax.broadcasted_iota(jnp.int32, vv.shape, 1) < ATTN_HEAD_DIM
    vj = jnp.where(low, vv, v_sw) if j == 0 else jnp.where(low, v_sw, vv)
    v_rep = jnp.concatenate([vj, vj], axis=1).astype(BF16)
    r_ref[o_row0:o_row0 + ATTN_GROUP * nq, :] = _dot(p, v_rep)
    head_of_lane = lax.broadcasted_iota(jnp.int32, (nq, gw), 1) // ATTN_HEAD_DIM
    og = None
    for h in range(ATTN_GROUP):
        oh = r_ref[o_row0 + h * nq:o_row0 + (h + 1) * nq, :] * (1.0 / denoms[h])
        oh = jnp.where(head_of_lane == h, oh, 0.0)
        og = oh if og is None else og + oh
    mix_ref[r0:r0 + nq, j * gw:(j + 1) * gw] = og.astype(BF16)


def _hgrn_gates(z_ref, lb, q_ref, k_ref, ghl_ref, rows):
    hq = z_ref[rows, COL_HQ:COL_HQ + HGRN_WIDTH] * (0.5 * HGRN_DIM ** -0.5)
    q_ref[rows, :] = hq + hq * jnp.tanh(z_ref[rows, COL_HQ:COL_HQ + HGRN_WIDTH] * 0.5)
    t = jnp.tanh(z_ref[rows, COL_HF:COL_HF + HGRN_WIDTH] * 0.5)
    half_span = 0.5 * (1.0 - lb)
    f = (1.0 - half_span) + half_span * t
    k_ref[rows, :] = half_span - half_span * t
    g = jnp.log(f) * LOG2E
    g_hi = g.astype(BF16)
    ghl_ref[rows, 0:HGRN_WIDTH] = g_hi
    ghl_ref[rows, HGRN_WIDTH:2 * HGRN_WIDTH] = (g - g_hi.astype(F32)).astype(BF16)


def _hgrn_cumsum(ghl_ref, cs_ref, bc_ref):
    tm = bc_ref.shape[0]
    grp = min(tm, 2 * LANES)
    ti = lax.broadcasted_iota(jnp.int32, (grp, grp), 0)
    si = lax.broadcasted_iota(jnp.int32, (grp, grp), 1)
    same_chunk = jnp.bitwise_xor(ti, si) < HGRN_CHUNK
    tri = ((si <= ti) & same_chunk).astype(BF16)
    for r0 in range(0, tm, grp):
        cs_ref[r0:r0 + grp, :] = _dot(tri, ghl_ref[r0:r0 + grp, :])
    bc_ref[...] = cs_ref[:, 0:HGRN_WIDTH] + cs_ref[:, HGRN_WIDTH:2 * HGRN_WIDTH]


def _hgrn_factors(c, q_ref, k_ref, bc_ref, qe_ref, kecat_ref, qd_ref, kd_ref):
    r0c = c * HGRN_CHUNK
    rows = slice(r0c, r0c + HGRN_CHUNK)
    bc = bc_ref[rows, :]
    q = q_ref[rows, :]
    k = k_ref[rows, :]
    for i in range(HGRN_NSUB):
        r0 = i * HGRN_SUB
        anchor = bc[r0 + HGRN_SUB // 2 - 1:r0 + HGRN_SUB // 2, :]
        qe_ref[r0c + r0:r0c + r0 + HGRN_SUB, :] = (q[r0:r0 + HGRN_SUB] * jnp.exp2(bc[r0:r0 + HGRN_SUB] - anchor)
                                                   ).astype(BF16)
        n_valid = r0 + HGRN_SUB
        ke = k[0:n_valid] * jnp.exp2(anchor - bc[0:n_valid])
        kecat_ref[c, i * HGRN_CHUNK:i * HGRN_CHUNK + n_valid, :] = ke.astype(BF16)
    qd_ref[rows, :] = (q * jnp.exp2(bc)).astype(BF16)
    b_last = bc[HGRN_CHUNK - 1:HGRN_CHUNK, :]
    e = c % 2
    kd_ref[c, e * HGRN_CHUNK:(e + 1) * HGRN_CHUNK, :] = (k * jnp.exp2(b_last - bc)).astype(BF16)
    kd_ref[c, (1 - e) * HGRN_CHUNK:(2 - e) * HGRN_CHUNK, :] = jnp.zeros((HGRN_CHUNK, HGRN_WIDTH), BF16)
    return jnp.exp2(b_last)


def _hgrn_scores(c, qe_ref, kecat_ref, r_ref, row0):
    rows = slice(c * HGRN_CHUNK, (c + 1) * HGRN_CHUNK)
    for h in range(HGRN_HEADS):
        hs = slice(h * HGRN_DIM, (h + 1) * HGRN_DIM)
        r_ref[row0 + h * HGRN_CHUNK:row0 + (h + 1) * HGRN_CHUNK, :] = _dot_nt(qe_ref[rows, hs], kecat_ref[c, :, hs])


def _hgrn_scores_finish(c, r_ref, row0, a_ref):
    rows = slice(c * HGRN_CHUNK, (c + 1) * HGRN_CHUNK)
    t = lax.broadcasted_iota(jnp.int32, (HGRN_CHUNK, LANES), 0)
    l = lax.broadcasted_iota(jnp.int32, (HGRN_CHUNK, LANES), 1)
    keep = ((l // HGRN_CHUNK) == ((t // HGRN_SUB) % 2)) & ((l % HGRN_CHUNK) <= t)
    half = HGRN_CHUNK // 2
    for h in range(HGRN_HEADS):
        hs = slice(h * HGRN_DIM, (h + 1) * HGRN_DIM)
        pr = row0 + h * HGRN_CHUNK
        a = jnp.concatenate([r_ref[pr:pr + half, 0:LANES], r_ref[pr + half:pr + HGRN_CHUNK, LANES:2 * LANES]], axis=0)
        a_ref[rows, hs] = jnp.where(keep, a, 0.0).astype(BF16)


def _pair_diag_mask():
    r = lax.broadcasted_iota(jnp.int32, (HGRN_PAIR, HGRN_PAIR), 0)
    l = lax.broadcasted_iota(jnp.int32, (HGRN_PAIR, HGRN_PAIR), 1)
    return (r // HGRN_DIM) == (l // HGRN_DIM)


def _hgrn_transpose_v(z_ref, vt_ref):
    eye = (lax.broadcasted_iota(jnp.int32, (LANES, LANES), 0)
           == lax.broadcasted_iota(jnp.int32, (LANES, LANES), 1)).astype(BF16)
    for a in range(HGRN_WIDTH // LANES):
        v = z_ref[:, COL_HI + a * LANES:COL_HI + (a + 1) * LANES].astype(BF16)
        vt_ref[a * LANES:(a + 1) * LANES, :] = _dot_nt(eye, v).astype(BF16)


def _hgrn_states(n_chunks, vt_ref, kd_ref, st_ref, stbf_ref, decays, r_ref, row0):
    for p in range(HGRN_HEADS // 2):
        ps = slice(p * HGRN_PAIR, (p + 1) * HGRN_PAIR)
        for c in range(n_chunks):
            tok = slice((c // 2) * LANES, (c // 2 + 1) * LANES)
            ur = row0 + (p * n_chunks + c) * HGRN_PAIR
            r_ref[ur:ur + HGRN_PAIR, :] = _dot(vt_ref[ps, tok], kd_ref[c, :, ps])
    for h in range(HGRN_HEADS):
        p, blk = h // 2, slice((h % 2) * HGRN_DIM, (h % 2 + 1) * HGRN_DIM)
        st = st_ref[h]
        for c in range(n_chunks):
            ur = row0 + (p * n_chunks + c) * HGRN_PAIR + (h % 2) * HGRN_DIM
            stbf_ref[c, p, blk, blk] = st.astype(BF16)
            st = st * decays[c][:, h * HGRN_DIM:(h + 1) * HGRN_DIM] + r_ref[ur:ur + HGRN_DIM, blk]
        st_ref[h] = st


def _hgrn_outputs(c, z_ref, a_ref, qd_ref, stbf_ref, onw, mix_ref, r_ref, row0):
    rows = slice(c * HGRN_CHUNK, (c + 1) * HGRN_CHUNK)
    diag = _pair_diag_mask()
    for p in range(HGRN_HEADS // 2):
        ps = slice(p * HGRN_PAIR, (p + 1) * HGRN_PAIR)
        vp = z_ref[rows, COL_HI + p * HGRN_PAIR:COL_HI + (p + 1) * HGRN_PAIR].astype(BF16)
        v_bd = jnp.where(diag, jnp.concatenate([vp] * (HGRN_PAIR // HGRN_CHUNK), axis=0), 0.0)
        orow = row0 + p * HGRN_CHUNK
        r_ref[orow:orow + HGRN_CHUNK, :] = _dot(a_ref[rows, ps], v_bd) + _dot_nt(qd_ref[rows, ps], stbf_ref[c, p])
        o = r_ref[orow:orow + HGRN_CHUNK, :]
        gr = z_ref[rows, COL_HG + p * HGRN_PAIR:COL_HG + (p + 1) * HGRN_PAIR]
        hg = 0.5 * gr
        gate = hg + hg * jnp.tanh(hg)
        outs = []
        for hh in range(2):
            oh = o[:, hh * HGRN_DIM:(hh + 1) * HGRN_DIM]
            outs.append(oh * lax.rsqrt(jnp.mean(oh * oh, axis=-1, keepdims=True) + EPS) * onw)
        rec = jnp.concatenate(outs, axis=1) * gate
        mix_ref[rows, ATTN_WIDTH + p * HGRN_PAIR:ATTN_WIDTH + (p + 1) * HGRN_PAIR] = rec.astype(BF16)


def _mixer_kernel(x_ref, prew_ref, win_ref, sink_ref, lbl_ref, onw_ref, wout_ref, postw_ref, o_ref,
                  z_ref, kv_ref, st_ref, mix_ref, q_ref, k_ref, ghl_ref, bc_ref, qe_ref, kecat_ref, qd_ref, kd_ref,
                  a_ref, vt_ref, stbf_ref, cs_ref, r_ref, h_ref, m_ref):
    n = pl.program_id(1)
    tm = x_ref.shape[1]
    ts = z_ref.shape[1]
    n_sub = tm // ts
    n_chunks = ts // HGRN_CHUNK
    units = [(blk, j) for blk in range(ts // ATTN_BLOCK) for j in range(ATTN_KV_HEADS)]

    @pl.when(n == 0)
    def _():
        kv_ref[0:ATTN_BLOCK, :] = jnp.zeros((ATTN_BLOCK, 2 * ATTN_KV_WIDTH), F32)
        st_ref[...] = jnp.zeros_like(st_ref)
        kecat_ref[...] = jnp.zeros_like(kecat_ref)
        stbf_ref[...] = jnp.zeros_like(stbf_ref)

    l0 = lbl_ref[0:1, :]
    l1 = lbl_ref[1:2, :]
    lm = jnp.maximum(l0, l1)
    e0 = jnp.exp(l0 - lm)
    lb = e0 / (e0 + jnp.exp(l1 - lm))

    unit_rows = ATTN_GROUP * ATTN_BLOCK
    s_base = 0
    pv_base = s_base + len(units) * unit_rows
    p_base = pv_base + len(units) * unit_rows
    upd_base = p_base + n_chunks * HGRN_HEADS * HGRN_CHUNK
    o_base = upd_base + n_chunks * HGRN_HEADS // 2 * HGRN_PAIR
    assert o_base + n_chunks * HGRN_HEADS // 2 * HGRN_CHUNK == r_ref.shape[0]

    def p_rows(c):
        return p_base + c * HGRN_HEADS * HGRN_CHUNK

    def o_rows(c):
        return o_base + c * HGRN_HEADS // 2 * HGRN_CHUNK

    def project_in(s):
        rows = slice(s * ts, (s + 1) * ts)
        z = z_ref.at[s % 2]

        def norm():
            h_ref[s % 2] = _rms(x_ref[0, rows, :], prew_ref[...]).astype(BF16)

        def cols(c0, c1):
            def step():
                z[:, c0:c1] = _dot(h_ref[s % 2], win_ref[:, c0:c1])
            return step

        def keep_kv():
            kv_ref[ATTN_BLOCK + s * ts:ATTN_BLOCK + (s + 1) * ts, :] = z[:, COL_K:COL_HQ]

        bounds = list(range(COL_HQ, IN_PROJ_WIDTH, IN_PROJ_STEP)) + [IN_PROJ_WIDTH]
        steps = [norm] + [cols(a, b) for a, b in zip(bounds[:-1], bounds[1:])]
        bounds = list(range(0, COL_HQ, IN_PROJ_STEP)) + [COL_HQ]
        return steps + [cols(a, b) for a, b in zip(bounds[:-1], bounds[1:])] + [keep_kv]

    def project_out(s):
        rows = slice(s * ts, (s + 1) * ts)

        def cols(c0, c1):
            def step():
                m_ref[s % 2, :, c0:c1] = _dot(mix_ref[s % 2], wout_ref[:, c0:c1])
            return step

        def finish():
            o_ref[0, rows, :] = x_ref[0, rows, :] + _rms(m_ref[s % 2], postw_ref[...])

        return [cols(c0, c0 + OUT_PROJ_STEP) for c0 in range(0, D_MODEL, OUT_PROJ_STEP)] + [finish]

    def mix_tokens(s, fill):
        z, mix = z_ref.at[s % 2], mix_ref.at[s % 2]
        q, k, ghl, bc, qe, kecat, qd, kd = q_ref, k_ref, ghl_ref, bc_ref, qe_ref, kecat_ref, qd_ref, kd_ref
        a, vt, stbf, cs, r = a_ref, vt_ref, stbf_ref, cs_ref, r_ref
        kv = kv_ref.at[pl.ds(s * ts, ATTN_BLOCK + ts)]

        def first_block(blk):
            return (n == 0) if (s == 0 and blk == 0) else jnp.bool_(False)

        fill(2)
        for c in range(n_chunks):
            _hgrn_gates(z, lb, q, k, ghl, slice(c * HGRN_CHUNK, (c + 1) * HGRN_CHUNK))
            if c % 2 == 1:
                fill()
        _hgrn_transpose_v(z, vt)
        _hgrn_cumsum(ghl, cs, bc)
        _swa_scores(z, kv, r, s_base, *units[0])
        decays = []
        for c in range(n_chunks):
            fill()
            decays.append(_hgrn_factors(c, q, k, bc, qe, kecat, qd, kd))
            _hgrn_scores(c, qe, kecat, r, p_rows(c))
        fill()
        _hgrn_states(n_chunks, vt, kd, st_ref, stbf, decays, r, upd_base)
        for c in range(n_chunks):
            _hgrn_scores_finish(c, r, p_rows(c), a)
        for u, (blk, j) in enumerate(units):
            fill()
            if u + 1 < len(units):
                _swa_scores(z, kv, r, s_base + (u + 1) * unit_rows, *units[u + 1])
            if u < n_chunks:
                _hgrn_outputs(u, z, a, qd, stbf, onw_ref[...], mix, r, o_rows(u))
            _swa_finish(r, s_base + u * unit_rows, pv_base + u * unit_rows, kv, sink_ref, mix, blk, j,
                        first_block(blk))
        for c in range(len(units), n_chunks):
            _hgrn_outputs(c, z, a, qd, stbf, onw_ref[...], mix, r, o_rows(c))
        fill(None)

    for step in project_in(0):
        step()
    for s in range(n_sub):
        nxt = project_in(s + 1) if s + 1 < n_sub else []
        prv = project_out(s - 1) if s > 0 else []
        pending = []
        while nxt or prv:
            if nxt:
                pending.append(nxt.pop(0))
            if prv:
                pending.append(prv.pop(0))

        def fill(count=1):
            for _ in range(len(pending) if count is None else min(count, len(pending))):
                pending.pop(0)()

        mix_tokens(s, fill)
    for step in project_out(n_sub - 1):
        step()
    kv_ref[0:ATTN_BLOCK, :] = kv_ref[tm:tm + ATTN_BLOCK, :]


def _mixer_result_rows(tm):
    units = (tm // ATTN_BLOCK) * ATTN_KV_HEADS
    chunks = tm // HGRN_CHUNK
    return (2 * units * ATTN_GROUP * ATTN_BLOCK + chunks * HGRN_HEADS * HGRN_CHUNK
            + chunks * (HGRN_HEADS // 2) * (HGRN_PAIR + HGRN_CHUNK))


def _mixer(x, pre_w, w_in, sinks, lb_logits, out_norm_w, w_out, post_w):
    b, t, d = x.shape
    tm, ts = TM_MIX, TS_MIX
    ns = tm // ts
    assert t % tm == 0 and d == D_MODEL and w_in.shape == (D_MODEL, IN_PROJ_WIDTH)
    assert tm % ts == 0 and ts % (2 * LANES) == 0
    full = lambda shape: pl.BlockSpec(shape, lambda i, j: (0,) * len(shape), pipeline_mode=pl.Buffered(1))
    return pl.pallas_call(
        _mixer_kernel,
        grid=(b, t // tm),
        in_specs=[
            pl.BlockSpec((1, tm, d), lambda i, j: (i, j, 0)),
            full((1, d)),
            full(w_in.shape),
            pl.BlockSpec(memory_space=pltpu.SMEM),
            full(lb_logits.shape),
            full((1, HGRN_DIM)),
            full(w_out.shape),
            full((1, d)),
        ],
        out_specs=pl.BlockSpec((1, tm, d), lambda i, j: (i, j, 0)),
        out_shape=jax.ShapeDtypeStruct((b, t, d), F32),
        scratch_shapes=[
            pltpu.VMEM((2, ts, IN_PROJ_WIDTH), F32),
            pltpu.VMEM((ATTN_BLOCK + tm, 2 * ATTN_KV_WIDTH), F32),
            pltpu.VMEM((HGRN_HEADS, HGRN_DIM, HGRN_DIM), F32),
            pltpu.VMEM((2, ts, ATTN_WIDTH + HGRN_WIDTH), BF16),
            pltpu.VMEM((ts, HGRN_WIDTH), F32),
            pltpu.VMEM((ts, HGRN_WIDTH), F32),
            pltpu.VMEM((ts, 2 * HGRN_WIDTH), BF16),
            pltpu.VMEM((ts, HGRN_WIDTH), F32),
            pltpu.VMEM((ts, HGRN_WIDTH), BF16),
            pltpu.VMEM((ts // HGRN_CHUNK, HGRN_NSUB * HGRN_CHUNK, HGRN_WIDTH), BF16),
            pltpu.VMEM((ts, HGRN_WIDTH), BF16),
            pltpu.VMEM((ts // HGRN_CHUNK, 2 * HGRN_CHUNK, HGRN_WIDTH), BF16),
            pltpu.VMEM((ts, HGRN_WIDTH), BF16),
            pltpu.VMEM((HGRN_WIDTH, ts), BF16),
            pltpu.VMEM((ts // HGRN_CHUNK, HGRN_HEADS // 2, HGRN_PAIR, HGRN_PAIR), BF16),
            pltpu.VMEM((ts, 2 * HGRN_WIDTH), F32),
            pltpu.VMEM((_mixer_result_rows(ts), HGRN_PAIR), F32),
            pltpu.VMEM((2, ts, D_MODEL), BF16),
            pltpu.VMEM((2, ts, D_MODEL), F32),
        ],
        compiler_params=pltpu.CompilerParams(
            dimension_semantics=("arbitrary", "arbitrary"), vmem_limit_bytes=VMEM_LIMIT),
        name="mixer",
    )(x, pre_w, w_in, sinks, lb_logits, out_norm_w, w_out, post_w)


def _xattn_kernel(x_ref, prew_ref, wq_ref, k_ref, v_ref, wo_ref, postw_ref, o_ref, cat_ref):
    x = x_ref[0]
    h = _rms(x, prew_ref[...]).astype(BF16)
    q = (_dot(h, wq_ref[...]) * (CA_HEAD_DIM ** -0.5 * LOG2E)).astype(BF16)

    def scores(hd):
        hs = slice(hd * CA_HEAD_DIM, (hd + 1) * CA_HEAD_DIM)
        return _dot_nt(q[:, hs], k_ref[0, :, hs])

    s_next = scores(0)
    for hd in range(CA_HEADS):
        hs = slice(hd * CA_HEAD_DIM, (hd + 1) * CA_HEAD_DIM)
        s = s_next
        if hd + 1 < CA_HEADS:
            s_next = scores(hd + 1)
        m = jnp.max(s, axis=-1, keepdims=True)
        p = jnp.exp2(s - m)
        p = (p * (1.0 / jnp.sum(p, axis=-1, keepdims=True))).astype(BF16)
        cat_ref[:, hs] = _dot(p, v_ref[0, :, hs]).astype(BF16)
    c = _dot(cat_ref[...], wo_ref[...])
    o_ref[0] = x + _rms(c, postw_ref[...])


def _xattn(x, pre_w, wq, k, v, wo, post_w):
    b, t, d = x.shape
    tm = TM_CA
    nm = k.shape[1]
    assert t % tm == 0
    full = lambda shape: pl.BlockSpec(shape, lambda i, j: (0,) * len(shape), pipeline_mode=pl.Buffered(1))
    return pl.pallas_call(
        _xattn_kernel,
        grid=(b, t // tm),
        in_specs=[
            pl.BlockSpec((1, tm, d), lambda i, j: (i, j, 0)),
            full((1, d)),
            full(wq.shape),
            pl.BlockSpec((1, nm, d), lambda i, j: (i, 0, 0)),
            pl.BlockSpec((1, nm, d), lambda i, j: (i, 0, 0)),
            full(wo.shape),
            full((1, d)),
        ],
        out_specs=pl.BlockSpec((1, tm, d), lambda i, j: (i, j, 0)),
        out_shape=jax.ShapeDtypeStruct((b, t, d), F32),
        scratch_shapes=[pltpu.VMEM((tm, d), BF16)],
        compiler_params=pltpu.CompilerParams(
            dimension_semantics=("arbitrary", "arbitrary"), vmem_limit_bytes=VMEM_LIMIT),
        name="xattn",
    )(x, pre_w, wq, k, v, wo, post_w)


def _gelu_tanh(x):
    return 0.5 * x * (1.0 + jnp.tanh(0.7978845608028654 * (x + 0.044715 * (x * x * x))))


def _causal_conv3(u_ref, kb, tm, cw_ref, cb_ref):
    ls = slice(kb * LANES, (kb + 1) * LANES)
    out = (cw_ref[2:3, ls] * u_ref[kb, HALO:HALO + tm, :] + cw_ref[1:2, ls] * u_ref[kb, HALO - 1:HALO - 1 + tm, :]
           + cw_ref[0:1, ls] * u_ref[kb, HALO - 2:HALO - 2 + tm, :]) + cb_ref[:, ls]
    u_ref[kb, 0:HALO, :] = u_ref[kb, tm:tm + HALO, :]
    return out


def _ffn_kernel(x_ref, prew_ref, wup_ref, cw_ref, cb_ref, wdn_ref, postw_ref, o_ref, u_ref, act_ref):
    n = pl.program_id(1)
    tm = x_ref.shape[1]
    n_chunks = D_FF // FF_CHUNK
    kb_per_chunk = FF_CHUNK // LANES
    kb_val = D_FF // LANES

    @pl.when(n == 0)
    def _():
        u_ref[:, 0:HALO, :] = jnp.zeros((u_ref.shape[0], HALO, LANES), F32)

    x = x_ref[0]
    h = _rms(x, prew_ref[...]).astype(BF16)

    def up(c):
        for base, col0 in ((0, 0), (kb_val, D_FF)):
            u = _dot(h, wup_ref[:, col0 + c * FF_CHUNK:col0 + (c + 1) * FF_CHUNK])
            for j in range(kb_per_chunk):
                u_ref[base + c * kb_per_chunk + j, HALO:HALO + tm, :] = u[:, j * LANES:(j + 1) * LANES]

    acc = None
    group_start = 0
    up(0)
    for c in range(n_chunks):
        if c + 1 < n_chunks:
            up(c + 1)
        for j in range(kb_per_chunk):
            kb = c * kb_per_chunk + j
            gate = _causal_conv3(u_ref, kb, tm, cw_ref, cb_ref)
            val = _causal_conv3(u_ref, kb_val + kb, tm, cw_ref, cb_ref)
            act_ref[:, kb * LANES:(kb + 1) * LANES] = (_gelu_tanh(gate) * val).astype(BF16)
        if (c + 1) % FF_DOWN_GROUP == 0 or c + 1 == n_chunks:
            ks = slice(group_start * FF_CHUNK, (c + 1) * FF_CHUNK)
            part = _dot(act_ref[:, ks], wdn_ref[ks, :])
            acc = part if acc is None else acc + part
            group_start = c + 1
    o_ref[0] = x + _rms(acc, postw_ref[...])


def _ffn(x, pre_w, w_up, conv_w, conv_b, w_down, post_w):
    b, t, d = x.shape
    tm = TM_FFN
    assert t % tm == 0 and w_up.shape == (D_MODEL, 2 * D_FF) and D_FF % FF_CHUNK == 0
    full = lambda shape: pl.BlockSpec(shape, lambda i, j: (0,) * len(shape), pipeline_mode=pl.Buffered(1))
    return pl.pallas_call(
        _ffn_kernel,
        grid=(b, t // tm),
        in_specs=[
            pl.BlockSpec((1, tm, d), lambda i, j: (i, j, 0)),
            full((1, d)),
            full(w_up.shape),
            full(conv_w.shape),
            full((1, 2 * D_FF)),
            full(w_down.shape),
            full((1, d)),
        ],
        out_specs=pl.BlockSpec((1, tm, d), lambda i, j: (i, j, 0)),
        out_shape=jax.ShapeDtypeStruct((b, t, d), F32),
        scratch_shapes=[pltpu.VMEM((2 * D_FF // LANES, HALO + tm, LANES), F32), pltpu.VMEM((tm, D_FF), BF16)],
        compiler_params=pltpu.CompilerParams(
            dimension_semantics=("arbitrary", "arbitrary"), vmem_limit_bytes=VMEM_LIMIT),
        name="ffn",
    )(x, pre_w, w_up, conv_w, conv_b, w_down, post_w)


def kernel(x, mem, mix_pre_norm, w_in, attn_sinks, hgrn_lb_logits, hgrn_out_norm, w_out, mix_post_norm, ca_pre_norm,
           mem_norm, ca_wq, ca_wk, ca_wv, ca_wo, ca_post_norm, ffn_pre_norm, ffn_w_up, ffn_conv_w, ffn_conv_b,
           ffn_w_down, ffn_post_norm):
    depth = w_in.shape[0]
    assert depth == 1 and hgrn_lb_logits.shape[0] == 2
    bf = lambda w: w.astype(BF16)
    for l in range(depth):
        x = _mixer(x, mix_pre_norm[l:l + 1], bf(w_in[l]), attn_sinks[l], hgrn_lb_logits, hgrn_out_norm[l:l + 1],
                   bf(w_out[l]), mix_post_norm[l:l + 1])
        k, v = _mem_kv(mem, mem_norm[l:l + 1], bf(ca_wk[l]), bf(ca_wv[l]))
        x = _xattn(x, ca_pre_norm[l:l + 1], bf(ca_wq[l]), k, v, bf(ca_wo[l]), ca_post_norm[l:l + 1])
        x = _ffn(x, ffn_pre_norm[l:l + 1], bf(ffn_w_up[l]), ffn_conv_w[l], ffn_conv_b[l:l + 1], bf(ffn_w_down[l]),
                 ffn_post_norm[l:l + 1])
    return x
```

```python
import functools

import jax
import jax.numpy as jnp
from jax import lax
from jax.experimental import pallas as pl
from jax.experimental.pallas import tpu as pltpu

F32 = jnp.float32
BF16 = jnp.bfloat16

EPS = 1e-6
LOG2E = 1.4426950408889634
D_MODEL = 1024
LANES = 128
BF16_SUBLANES = 16
ATTN_WIDTH = 512
ATTN_HEAD_DIM = 64
ATTN_GROUP = 4
ATTN_KV_HEADS = 2
ATTN_KV_WIDTH = ATTN_KV_HEADS * ATTN_HEAD_DIM
ATTN_BLOCK = 128
HGRN_HEADS = 4
HGRN_DIM = 128
HGRN_WIDTH = HGRN_HEADS * HGRN_DIM
HGRN_PAIR = 2 * HGRN_DIM
COL_Q = 0
COL_K = COL_Q + ATTN_WIDTH
COL_V = COL_K + ATTN_KV_WIDTH
COL_HQ = COL_V + ATTN_KV_WIDTH
COL_HF = COL_HQ + HGRN_WIDTH
COL_HI = COL_HF + HGRN_WIDTH
COL_HG = COL_HI + HGRN_WIDTH
IN_PROJ_WIDTH = COL_HG + HGRN_WIDTH
IN_PROJ_STEP = 512
OUT_PROJ_STEP = 512
HGRN_CHUNK = 64
HGRN_SUB = 16
HGRN_NSUB = HGRN_CHUNK // HGRN_SUB
CA_HEADS = 4
CA_HEAD_DIM = 256
D_FF = 2816
FF_CHUNK = 256
FF_DOWN_GROUP = 4
HALO = 8

TM_MIX = 512
TS_MIX = 256
TM_CA = 1024
TM_FFN = 512

VMEM_LIMIT = 56 * 1024 * 1024

NT_DIMS = (((1,), (1,)), ((), ()))
TN_DIMS = (((0,), (0,)), ((), ()))


def _rms(x, w):
    return x * lax.rsqrt(jnp.mean(x * x, axis=-1, keepdims=True) + EPS) * w


def _dot(a, b):
    return jnp.dot(a, b, preferred_element_type=F32)


def _dot_nt(a, b):
    return lax.dot_general(a, b, NT_DIMS, preferred_element_type=F32)


def _dot_tn(a, b):
    return lax.dot_general(a, b, TN_DIMS, preferred_element_type=F32)


def _mem_kv_kernel(mem_ref, nw_ref, wk_ref, wv_ref, k_ref, v_ref):
    mn = _rms(mem_ref[0], nw_ref[...]).astype(BF16)
    k_ref[0] = _dot(mn, wk_ref[...]).astype(BF16)
    v_ref[0] = _dot(mn, wv_ref[...]).astype(BF16)


def _mem_kv(mem, norm_w, wk, wv):
    b, m, d = mem.shape
    full = lambda shape: pl.BlockSpec(shape, lambda i: (0,) * len(shape))
    return pl.pallas_call(
        _mem_kv_kernel,
        grid=(b,),
        in_specs=[pl.BlockSpec((1, m, d), lambda i: (i, 0, 0)), full((1, d)), full(wk.shape), full(wv.shape)],
        out_specs=[pl.BlockSpec((1, m, d), lambda i: (i, 0, 0))] * 2,
        out_shape=[jax.ShapeDtypeStruct((b, m, d), BF16)] * 2,
        compiler_params=pltpu.CompilerParams(dimension_semantics=("arbitrary",), vmem_limit_bytes=VMEM_LIMIT),
        name="mem_kv",
    )(mem, norm_w, wk, wv)


def _swa_scores(z_ref, kv_ref, r_ref, row0, blk, j):
    nq = ATTN_BLOCK
    gw = ATTN_GROUP * ATTN_HEAD_DIM
    r0 = blk * nq
    q = z_ref[r0:r0 + nq, COL_Q + j * gw:COL_Q + (j + 1) * gw] * (ATTN_HEAD_DIM ** -0.5 * LOG2E)
    kk = kv_ref[r0:r0 + 2 * nq, 0:ATTN_KV_WIDTH]
    k_sw = pltpu.roll(kk, ATTN_HEAD_DIM, axis=1)
    low = lax.broadcasted_iota(jnp.int32, kk.shape, 1) < ATTN_HEAD_DIM
    kj = jnp.where(low, kk, k_sw) if j == 0 else jnp.where(low, k_sw, kk)
    k_rep = jnp.concatenate([kj, kj], axis=1).astype(BF16)
    head_of_lane = lax.broadcasted_iota(jnp.int32, (nq, gw), 1) // ATTN_HEAD_DIM
    qs = jnp.concatenate([jnp.where(head_of_lane == h, q, 0.0) for h in range(ATTN_GROUP)], axis=0).astype(BF16)
    r_ref[row0:row0 + ATTN_GROUP * nq, :] = _dot_nt(qs, k_rep)


def _swa_finish(r_ref, s_row0, o_row0, kv_ref, sink_ref, mix_ref, blk, j, first_block):
    nq = ATTN_BLOCK
    s = r_ref[s_row0:s_row0 + ATTN_GROUP * nq, :]
    gw = ATTN_GROUP * ATTN_HEAD_DIM
    r0 = blk * nq
    qi = lax.broadcasted_iota(jnp.int32, (nq, 2 * nq), 0)
    kj = lax.broadcasted_iota(jnp.int32, (nq, 2 * nq), 1)
    diff = qi + nq - kj
    allowed = (diff >= 0) & (diff < nq) & ((kj >= nq) | jnp.logical_not(first_block))
    ps, denoms = [], []
    for h in range(ATTN_GROUP):
        sink = sink_ref[j * ATTN_GROUP + h] * LOG2E
        sh = jnp.where(allowed, s[h * nq:(h + 1) * nq], -jnp.inf)
        mh = jnp.maximum(jnp.max(sh, axis=-1, keepdims=True), sink)
        ph = jnp.exp2(sh - mh)
        ps.append(ph.astype(BF16))
        denoms.append(jnp.sum(ph, axis=-1, keepdims=True) + jnp.exp2(sink - mh))
    p = jnp.concatenate(ps, axis=0)

    vv = kv_ref[r0:r0 + 2 * nq, ATTN_KV_WIDTH:2 * ATTN_KV_WIDTH]
    v_sw = pltpu.roll(vv, ATTN_HEAD_DIM, axis=1)
    low = lax.broadcasted_iota(jnp.int32, vv.shape, 1) < ATTN_HEAD_DIM
    vj = jnp.where(low, vv, v_sw) if j == 0 else jnp.where(low, v_sw, vv)
    v_rep = jnp.concatenate([vj, vj], axis=1).astype(BF16)
    r_ref[o_row0:o_row0 + ATTN_GROUP * nq, :] = _dot(p, v_rep)
    head_of_lane = lax.broadcasted_iota(jnp.int32, (nq, gw), 1) // ATTN_HEAD_DIM
    og = None
    for h in range(ATTN_GROUP):
        oh = r_ref[o_row0 + h * nq:o_row0 + (h + 1) * nq, :] * (1.0 / denoms[h])
        oh = jnp.where(head_of_lane == h, oh, 0.0)
        og = oh if og is None else og + oh
    mix_ref[r0:r0 + nq, j * gw:(j + 1) * gw] = og.astype(BF16)


def _hgrn_gates(z_ref, lb, q_ref, k_ref, ghl_ref, rows):
    hq = z_ref[rows, COL_HQ:COL_HQ + HGRN_WIDTH] * (0.5 * HGRN_DIM ** -0.5)
    q_ref[rows, :] = hq + hq * jnp.tanh(z_ref[rows, COL_HQ:COL_HQ + HGRN_WIDTH] * 0.5)
    t = jnp.tanh(z_ref[rows, COL_HF:COL_HF + HGRN_WIDTH] * 0.5)
    half_span = 0.5 * (1.0 - lb)
    f = (1.0 - half_span) + half_span * t
    k_ref[rows, :] = half_span - half_span * t
    g = jnp.log(f) * LOG2E
    g_hi = g.astype(BF16)
    ghl_ref[rows, 0:HGRN_WIDTH] = g_hi
    ghl_ref[rows, HGRN_WIDTH:2 * HGRN_WIDTH] = (g - g_hi.astype(F32)).astype(BF16)


def _hgrn_cumsum(ghl_ref, cs_ref, bc_ref):
    tm = bc_ref.shape[0]
    grp = min(tm, 2 * LANES)
    ti = lax.broadcasted_iota(jnp.int32, (grp, grp), 0)
    si = lax.broadcasted_iota(jnp.int32, (grp, grp), 1)
    same_chunk = jnp.bitwise_xor(ti, si) < HGRN_CHUNK
    tri = ((si <= ti) & same_chunk).astype(BF16)
    for r0 in range(0, tm, grp):
        cs_ref[r0:r0 + grp, :] = _dot(tri, ghl_ref[r0:r0 + grp, :])
    bc_ref[...] = cs_ref[:, 0:HGRN_WIDTH] + cs_ref[:, HGRN_WIDTH:2 * HGRN_WIDTH]


def _hgrn_factors(c, q_ref, k_ref, bc_ref, qe_ref, kecat_ref, qd_ref, kd_ref):
    r0c = c * HGRN_CHUNK
    rows = slice(r0c, r0c + HGRN_CHUNK)
    bc = bc_ref[rows, :]
    q = q_ref[rows, :]
    k = k_ref[rows, :]
    for i in range(HGRN_NSUB):
        r0 = i * HGRN_SUB
        anchor = bc[r0 + HGRN_SUB // 2 - 1:r0 + HGRN_SUB // 2, :]
        qe_ref[r0c + r0:r0c + r0 + HGRN_SUB, :] = (q[r0:r0 + HGRN_SUB] * jnp.exp2(bc[r0:r0 + HGRN_SUB] - anchor)
                                                   ).astype(BF16)
        n_valid = r0 + HGRN_SUB
        ke = k[0:n_valid] * jnp.exp2(anchor - bc[0:n_valid])
        kecat_ref[c, i * HGRN_CHUNK:i * HGRN_CHUNK + n_valid, :] = ke.astype(BF16)
    qd_ref[rows, :] = (q * jnp.exp2(bc)).astype(BF16)
    b_last = bc[HGRN_CHUNK - 1:HGRN_CHUNK, :]
    e = c % 2
    kd_ref[c, e * HGRN_CHUNK:(e + 1) * HGRN_CHUNK, :] = (k * jnp.exp2(b_last - bc)).astype(BF16)
    kd_ref[c, (1 - e) * HGRN_CHUNK:(2 - e) * HGRN_CHUNK, :] = jnp.zeros((HGRN_CHUNK, HGRN_WIDTH), BF16)
    return jnp.exp2(b_last)


def _hgrn_scores(c, qe_ref, kecat_ref, r_ref, row0):
    rows = slice(c * HGRN_CHUNK, (c + 1) * HGRN_CHUNK)
    for h in range(HGRN_HEADS):
        hs = slice(h * HGRN_DIM, (h + 1) * HGRN_DIM)
        r_ref[row0 + h * HGRN_CHUNK:row0 + (h + 1) * HGRN_CHUNK, :] = _dot_nt(qe_ref[rows, hs], kecat_ref[c, :, hs])


def _hgrn_scores_finish(c, r_ref, row0, a_ref):
    rows = slice(c * HGRN_CHUNK, (c + 1) * HGRN_CHUNK)
    t = lax.broadcasted_iota(jnp.int32, (HGRN_CHUNK, LANES), 0)
    l = lax.broadcasted_iota(jnp.int32, (HGRN_CHUNK, LANES), 1)
    keep = ((l // HGRN_CHUNK) == ((t // HGRN_SUB) % 2)) & ((l % HGRN_CHUNK) <= t)
    half = HGRN_CHUNK // 2
    for h in range(HGRN_HEADS):
        hs = slice(h * HGRN_DIM, (h + 1) * HGRN_DIM)
        pr = row0 + h * HGRN_CHUNK
        a = jnp.concatenate([r_ref[pr:pr + half, 0:LANES], r_ref[pr + half:pr + HGRN_CHUNK, LANES:2 * LANES]], axis=0)
        a_ref[rows, hs] = jnp.where(keep, a, 0.0).astype(BF16)


def _pair_diag_mask():
    r = lax.broadcasted_iota(jnp.int32, (HGRN_PAIR, HGRN_PAIR), 0)
    l = lax.broadcasted_iota(jnp.int32, (HGRN_PAIR, HGRN_PAIR), 1)
    return (r // HGRN_DIM) == (l // HGRN_DIM)


def _hgrn_transpose_v(z_ref, vt_ref):
    eye = (lax.broadcasted_iota(jnp.int32, (LANES, LANES), 0)
           == lax.broadcasted_iota(jnp.int32, (LANES, LANES), 1)).astype(BF16)
    for a in range(HGRN_WIDTH // LANES):
        v = z_ref[:, COL_HI + a * LANES:COL_HI + (a + 1) * LANES].astype(BF16)
        vt_ref[a * LANES:(a + 1) * LANES, :] = _dot_nt(eye, v).astype(BF16)


def _hgrn_states(n_chunks, vt_ref, kd_ref, st_ref, stbf_ref, decays, r_ref, row0):
    for p in range(HGRN_HEADS // 2):
        ps = slice(p * HGRN_PAIR, (p + 1) * HGRN_PAIR)
        for c in range(n_chunks):
            tok = slice((c // 2) * LANES, (c // 2 + 1) * LANES)
            ur = row0 + (p * n_chunks + c) * HGRN_PAIR
            r_ref[ur:ur + HGRN_PAIR, :] = _dot(vt_ref[ps, tok], kd_ref[c, :, ps])
    for h in range(HGRN_HEADS):
        p, blk = h // 2, slice((h % 2) * HGRN_DIM, (h % 2 + 1) * HGRN_DIM)
        st = st_ref[h]
        for c in range(n_chunks):
            ur = row0 + (p * n_chunks + c) * HGRN_PAIR + (h % 2) * HGRN_DIM
            stbf_ref[c, p, blk, blk] = st.astype(BF16)
            st = st * decays[c][:, h * HGRN_DIM:(h + 1) * HGRN_DIM] + r_ref[ur:ur + HGRN_DIM, blk]
        st_ref[h] = st


def _hgrn_outputs(c, z_ref, a_ref, qd_ref, stbf_ref, onw, mix_ref, r_ref, row0):
    rows = slice(c * HGRN_CHUNK, (c + 1) * HGRN_CHUNK)
    diag = _pair_diag_mask()
    for p in range(HGRN_HEADS // 2):
        ps = slice(p * HGRN_PAIR, (p + 1) * HGRN_PAIR)
        vp = z_ref[rows, COL_HI + p * HGRN_PAIR:COL_HI + (p + 1) * HGRN_PAIR].astype(BF16)
        v_bd = jnp.where(diag, jnp.concatenate([vp] * (HGRN_PAIR // HGRN_CHUNK), axis=0), 0.0)
        orow = row0 + p * HGRN_CHUNK
        r_ref[orow:orow + HGRN_CHUNK, :] = _dot(a_ref[rows, ps], v_bd) + _dot_nt(qd_ref[rows, ps], stbf_ref[c, p])
        o = r_ref[orow:orow + HGRN_CHUNK, :]
        gr = z_ref[rows, COL_HG + p * HGRN_PAIR:COL_HG + (p + 1) * HGRN_PAIR]
        hg = 0.5 * gr
        gate = hg + hg * jnp.tanh(hg)
        outs = []
        for hh in range(2):
            oh = o[:, hh * HGRN_DIM:(hh + 1) * HGRN_DIM]
            outs.append(oh * lax.rsqrt(jnp.mean(oh * oh, axis=-1, keepdims=True) + EPS) * onw)
        rec = jnp.concatenate(outs, axis=1) * gate
        mix_ref[rows, ATTN_WIDTH + p * HGRN_PAIR:ATTN_WIDTH + (p + 1) * HGRN_PAIR] = rec.astype(BF16)


def _mixer_kernel(n_cast, x_ref, prew_ref, win_ref, sink_ref, lbl_ref, onw_ref, wout_ref, postw_ref, *rest):
    cast_in, o_ref, cast_out = rest[:n_cast], rest[n_cast], rest[n_cast + 1:2 * n_cast + 1]
    (z_ref, kv_ref, st_ref, mix_ref, q_ref, k_ref, ghl_ref, bc_ref, qe_ref, kecat_ref, qd_ref, kd_ref,
     a_ref, vt_ref, stbf_ref, cs_ref, r_ref, h_ref, m_ref) = rest[2 * n_cast + 1:]
    for src, dst in zip(cast_in, cast_out):
        dst[...] = src[...].astype(BF16)
    n = pl.program_id(1)
    tm = x_ref.shape[1]
    ts = z_ref.shape[1]
    n_sub = tm // ts
    n_chunks = ts // HGRN_CHUNK
    units = [(blk, j) for blk in range(ts // ATTN_BLOCK) for j in range(ATTN_KV_HEADS)]

    @pl.when(n == 0)
    def _():
        kv_ref[0:ATTN_BLOCK, :] = jnp.zeros((ATTN_BLOCK, 2 * ATTN_KV_WIDTH), F32)
        st_ref[...] = jnp.zeros_like(st_ref)
        kecat_ref[...] = jnp.zeros_like(kecat_ref)
        stbf_ref[...] = jnp.zeros_like(stbf_ref)

    l0 = lbl_ref[0:1, :]
    l1 = lbl_ref[1:2, :]
    lm = jnp.maximum(l0, l1)
    e0 = jnp.exp(l0 - lm)
    lb = e0 / (e0 + jnp.exp(l1 - lm))

    unit_rows = ATTN_GROUP * ATTN_BLOCK
    s_base = 0
    pv_base = s_base + len(units) * unit_rows
    p_base = pv_base + len(units) * unit_rows
    upd_base = p_base + n_chunks * HGRN_HEADS * HGRN_CHUNK
    o_base = upd_base + n_chunks * HGRN_HEADS // 2 * HGRN_PAIR
    assert o_base + n_chunks * HGRN_HEADS // 2 * HGRN_CHUNK == r_ref.shape[0]

    def p_rows(c):
        return p_base + c * HGRN_HEADS * HGRN_CHUNK

    def o_rows(c):
        return o_base + c * HGRN_HEADS // 2 * HGRN_CHUNK

    def project_in(s):
        rows = slice(s * ts, (s + 1) * ts)
        z = z_ref.at[s % 2]

        def norm():
            h_ref[s % 2] = _rms(x_ref[0, rows, :], prew_ref[...]).astype(BF16)

        def cols(c0, c1):
            def step():
                z[:, c0:c1] = _dot(h_ref[s % 2], win_ref[:, c0:c1])
            return step

        def keep_kv():
            kv_ref[ATTN_BLOCK + s * ts:ATTN_BLOCK + (s + 1) * ts, :] = z[:, COL_K:COL_HQ]

        bounds = list(range(COL_HQ, IN_PROJ_WIDTH, IN_PROJ_STEP)) + [IN_PROJ_WIDTH]
        hgrn_steps = [norm] + [cols(a, b) for a, b in zip(bounds[:-1], bounds[1:])]
        bounds = list(range(0, COL_HQ, IN_PROJ_STEP)) + [COL_HQ]
        return hgrn_steps, [cols(a, b) for a, b in zip(bounds[:-1], bounds[1:])] + [keep_kv]

    def project_out(s):
        rows = slice(s * ts, (s + 1) * ts)

        def cols(c0, c1):
            def step():
                m_ref[s % 2, :, c0:c1] = _dot(mix_ref[s % 2], wout_ref[:, c0:c1])
            return step

        def finish():
            o_ref[0, rows, :] = x_ref[0, rows, :] + _rms(m_ref[s % 2], postw_ref[...])

        return [cols(c0, c0 + OUT_PROJ_STEP) for c0 in range(0, D_MODEL, OUT_PROJ_STEP)] + [finish]

    def mix_tokens(s, fill, fill_own):
        z, mix = z_ref.at[s % 2], mix_ref.at[s % 2]
        q, k, ghl, bc, qe, kecat, qd, kd = q_ref, k_ref, ghl_ref, bc_ref, qe_ref, kecat_ref, qd_ref, kd_ref
        a, vt, stbf, cs, r = a_ref, vt_ref, stbf_ref, cs_ref, r_ref
        kv = kv_ref.at[pl.ds(s * ts, ATTN_BLOCK + ts)]

        def first_block(blk):
            return (n == 0) if (s == 0 and blk == 0) else jnp.bool_(False)

        fill(2)
        for c in range(n_chunks):
            _hgrn_gates(z, lb, q, k, ghl, slice(c * HGRN_CHUNK, (c + 1) * HGRN_CHUNK))
            if c % 2 == 1:
                fill()
        _hgrn_transpose_v(z, vt)
        _hgrn_cumsum(ghl, cs, bc)
        fill_own()
        _swa_scores(z, kv, r, s_base, *units[0])
        decays = []
        for c in range(n_chunks):
            fill()
            decays.append(_hgrn_factors(c, q, k, bc, qe, kecat, qd, kd))
            _hgrn_scores(c, qe, kecat, r, p_rows(c))
        fill()
        _hgrn_states(n_chunks, vt, kd, st_ref, stbf, decays, r, upd_base)
        for c in range(n_chunks):
            _hgrn_scores_finish(c, r, p_rows(c), a)
        for u, (blk, j) in enumerate(units):
            fill()
            if u + 1 < len(units):
                _swa_scores(z, kv, r, s_base + (u + 1) * unit_rows, *units[u + 1])
            if u < n_chunks:
                _hgrn_outputs(u, z, a, qd, stbf, onw_ref[...], mix, r, o_rows(u))
            _swa_finish(r, s_base + u * unit_rows, pv_base + u * unit_rows, kv, sink_ref, mix, blk, j,
                        first_block(blk))
        for c in range(len(units), n_chunks):
            _hgrn_outputs(c, z, a, qd, stbf, onw_ref[...], mix, r, o_rows(c))
        fill(None)

    hgrn_steps, own = project_in(0)
    for step in hgrn_steps:
        step()
    for s in range(n_sub):
        nxt, nxt_own = project_in(s + 1) if s + 1 < n_sub else ([], [])
        prv = project_out(s - 1) if s > 0 else []
        pending = list(own)
        while nxt or prv:
            if nxt:
                pending.append(nxt.pop(0))
            if prv:
                pending.append(prv.pop(0))
        own_left = [len(own)]

        def fill(count=1):
            for _ in range(len(pending) if count is None else min(count, len(pending))):
                pending.pop(0)()
                own_left[0] -= 1

        def fill_own():
            fill(max(own_left[0], 0))

        mix_tokens(s, fill, fill_own)
        own = nxt_own
    for step in project_out(n_sub - 1):
        step()
    kv_ref[0:ATTN_BLOCK, :] = kv_ref[tm:tm + ATTN_BLOCK, :]


def _mixer_result_rows(tm):
    units = (tm // ATTN_BLOCK) * ATTN_KV_HEADS
    chunks = tm // HGRN_CHUNK
    return (2 * units * ATTN_GROUP * ATTN_BLOCK + chunks * HGRN_HEADS * HGRN_CHUNK
            + chunks * (HGRN_HEADS // 2) * (HGRN_PAIR + HGRN_CHUNK))


def _cast_block_spec(w, b, nt):
    rows, cols = w.shape
    assert rows % (b * nt * BF16_SUBLANES) == 0
    return pl.BlockSpec((rows // (b * nt), cols), lambda i, j: (i * nt + j, 0))


def _mixer(x, pre_w, w_in, sinks, lb_logits, out_norm_w, w_out, post_w, later_weights):
    b, t, d = x.shape
    tm, ts = TM_MIX, TS_MIX
    ns = tm // ts
    nt = t // tm
    assert t % tm == 0 and d == D_MODEL and w_in.shape == (D_MODEL, IN_PROJ_WIDTH)
    assert tm % ts == 0 and ts % (2 * LANES) == 0
    full = lambda shape: pl.BlockSpec(shape, lambda i, j: (0,) * len(shape), pipeline_mode=pl.Buffered(1))
    cast_specs = [_cast_block_spec(w, b, nt) for w in later_weights]
    outs = pl.pallas_call(
        functools.partial(_mixer_kernel, len(later_weights)),
        grid=(b, nt),
        in_specs=[
            pl.BlockSpec((1, tm, d), lambda i, j: (i, j, 0)),
            full((1, d)),
            full(w_in.shape),
            pl.BlockSpec(memory_space=pltpu.SMEM),
            full(lb_logits.shape),
            full((1, HGRN_DIM)),
            full(w_out.shape),
            full((1, d)),
        ] + cast_specs,
        out_specs=[pl.BlockSpec((1, tm, d), lambda i, j: (i, j, 0))] + cast_specs,
        out_shape=[jax.ShapeDtypeStruct((b, t, d), F32)] + [jax.ShapeDtypeStruct(w.shape, BF16) for w in later_weights],
        scratch_shapes=[
            pltpu.VMEM((2, ts, IN_PROJ_WIDTH), F32),
            pltpu.VMEM((ATTN_BLOCK + tm, 2 * ATTN_KV_WIDTH), F32),
            pltpu.VMEM((HGRN_HEADS, HGRN_DIM, HGRN_DIM), F32),
            pltpu.VMEM((2, ts, ATTN_WIDTH + HGRN_WIDTH), BF16),
            pltpu.VMEM((ts, HGRN_WIDTH), F32),
            pltpu.VMEM((ts, HGRN_WIDTH), F32),
            pltpu.VMEM((ts, 2 * HGRN_WIDTH), BF16),
            pltpu.VMEM((ts, HGRN_WIDTH), F32),
            pltpu.VMEM((ts, HGRN_WIDTH), BF16),
            pltpu.VMEM((ts // HGRN_CHUNK, HGRN_NSUB * HGRN_CHUNK, HGRN_WIDTH), BF16),
            pltpu.VMEM((ts, HGRN_WIDTH), BF16),
            pltpu.VMEM((ts // HGRN_CHUNK, 2 * HGRN_CHUNK, HGRN_WIDTH), BF16),
            pltpu.VMEM((ts, HGRN_WIDTH), BF16),
            pltpu.VMEM((HGRN_WIDTH, ts), BF16),
            pltpu.VMEM((ts // HGRN_CHUNK, HGRN_HEADS // 2, HGRN_PAIR, HGRN_PAIR), BF16),
            pltpu.VMEM((ts, 2 * HGRN_WIDTH), F32),
            pltpu.VMEM((_mixer_result_rows(ts), HGRN_PAIR), F32),
            pltpu.VMEM((2, ts, D_MODEL), BF16),
            pltpu.VMEM((2, ts, D_MODEL), F32),
        ],
        compiler_params=pltpu.CompilerParams(
            dimension_semantics=("arbitrary", "arbitrary"), vmem_limit_bytes=VMEM_LIMIT),
        name="mixer",
    )(x, pre_w, w_in, sinks, lb_logits, out_norm_w, w_out, post_w, *later_weights)
    return outs[0], outs[1:]


def _xattn_kernel(x_ref, prew_ref, wq_ref, k_ref, v_ref, wo_ref, postw_ref, o_ref, cat_ref):
    x = x_ref[0]
    h = _rms(x, prew_ref[...]).astype(BF16)
    q = (_dot(h, wq_ref[...]) * (CA_HEAD_DIM ** -0.5 * LOG2E)).astype(BF16)

    def scores(hd):
        hs = slice(hd * CA_HEAD_DIM, (hd + 1) * CA_HEAD_DIM)
        return _dot_nt(q[:, hs], k_ref[0, :, hs])

    s_next = scores(0)
    for hd in range(CA_HEADS):
        hs = slice(hd * CA_HEAD_DIM, (hd + 1) * CA_HEAD_DIM)
        s = s_next
        if hd + 1 < CA_HEADS:
            s_next = scores(hd + 1)
        m = jnp.max(s, axis=-1, keepdims=True)
        p = jnp.exp2(s - m)
        p = (p * (1.0 / jnp.sum(p, axis=-1, keepdims=True))).astype(BF16)
        cat_ref[:, hs] = _dot(p, v_ref[0, :, hs]).astype(BF16)
    half = x.shape[0] // 2
    for rows in (slice(0, half), slice(half, 2 * half)):
        c = _dot(cat_ref[rows, :], wo_ref[...])
        o_ref[0, rows, :] = x[rows] + _rms(c, postw_ref[...])


def _xattn(x, pre_w, wq, k, v, wo, post_w):
    b, t, d = x.shape
    tm = TM_CA
    nm = k.shape[1]
    assert t % tm == 0
    full = lambda shape: pl.BlockSpec(shape, lambda i, j: (0,) * len(shape), pipeline_mode=pl.Buffered(1))
    return pl.pallas_call(
        _xattn_kernel,
        grid=(b, t // tm),
        in_specs=[
            pl.BlockSpec((1, tm, d), lambda i, j: (i, j, 0)),
            full((1, d)),
            full(wq.shape),
            pl.BlockSpec((1, nm, d), lambda i, j: (i, 0, 0)),
            pl.BlockSpec((1, nm, d), lambda i, j: (i, 0, 0)),
            full(wo.shape),
            full((1, d)),
        ],
        out_specs=pl.BlockSpec((1, tm, d), lambda i, j: (i, j, 0)),
        out_shape=jax.ShapeDtypeStruct((b, t, d), F32),
        scratch_shapes=[pltpu.VMEM((tm, d), BF16)],
        compiler_params=pltpu.CompilerParams(
            dimension_semantics=("arbitrary", "arbitrary"), vmem_limit_bytes=VMEM_LIMIT),
        name="xattn",
    )(x, pre_w, wq, k, v, wo, post_w)


def _gelu_tanh(x):
    return 0.5 * x * (1.0 + jnp.tanh(0.7978845608028654 * (x + 0.044715 * (x * x * x))))


def _causal_conv3(u_ref, kb, tm, cw_ref, cb_ref):
    ls = slice(kb * LANES, (kb + 1) * LANES)
    out = (cw_ref[2:3, ls] * u_ref[kb, HALO:HALO + tm, :] + cw_ref[1:2, ls] * u_ref[kb, HALO - 1:HALO - 1 + tm, :]
           + cw_ref[0:1, ls] * u_ref[kb, HALO - 2:HALO - 2 + tm, :]) + cb_ref[:, ls]
    u_ref[kb, 0:HALO, :] = u_ref[kb, tm:tm + HALO, :]
    return out


def _ffn_kernel(x_ref, prew_ref, wup_ref, cw_ref, cb_ref, wdn_ref, postw_ref, o_ref, u_ref, act_ref):
    n = pl.program_id(1)
    tm = x_ref.shape[1]
    n_chunks = D_FF // FF_CHUNK
    kb_per_chunk = FF_CHUNK // LANES
    kb_val = D_FF // LANES

    @pl.when(n == 0)
    def _():
        u_ref[:, 0:HALO, :] = jnp.zeros((u_ref.shape[0], HALO, LANES), F32)

    x = x_ref[0]
    h = _rms(x, prew_ref[...]).astype(BF16)

    def up(c):
        for base, col0 in ((0, 0), (kb_val, D_FF)):
            u = _dot(h, wup_ref[:, col0 + c * FF_CHUNK:col0 + (c + 1) * FF_CHUNK])
            for j in range(kb_per_chunk):
                u_ref[base + c * kb_per_chunk + j, HALO:HALO + tm, :] = u[:, j * LANES:(j + 1) * LANES]

    acc = None
    group_start = 0
    up(0)
    for c in range(n_chunks):
        if c + 1 < n_chunks:
            up(c + 1)
        for j in range(kb_per_chunk):
            kb = c * kb_per_chunk + j
            gate = _causal_conv3(u_ref, kb, tm, cw_ref, cb_ref)
            val = _causal_conv3(u_ref, kb_val + kb, tm, cw_ref, cb_ref)
            act_ref[:, kb * LANES:(kb + 1) * LANES] = (_gelu_tanh(gate) * val).astype(BF16)
        if (c + 1) % FF_DOWN_GROUP == 0 or c + 1 == n_chunks:
            ks = slice(group_start * FF_CHUNK, (c + 1) * FF_CHUNK)
            part = _dot(act_ref[:, ks], wdn_ref[ks, :])
            acc = part if acc is None else acc + part
            group_start = c + 1
    o_ref[0] = x + _rms(acc, postw_ref[...])


def _ffn(x, pre_w, w_up, conv_w, conv_b, w_down, post_w):
    b, t, d = x.shape
    tm = TM_FFN
    assert t % tm == 0 and w_up.shape == (D_MODEL, 2 * D_FF) and D_FF % FF_CHUNK == 0
    full = lambda shape: pl.BlockSpec(shape, lambda i, j: (0,) * len(shape), pipeline_mode=pl.Buffered(1))
    return pl.pallas_call(
        _ffn_kernel,
        grid=(b, t // tm),
        in_specs=[
            pl.BlockSpec((1, tm, d), lambda i, j: (i, j, 0)),
            full((1, d)),
            full(w_up.shape),
            full(conv_w.shape),
            full((1, 2 * D_FF)),
            full(w_down.shape),
            full((1, d)),
        ],
        out_specs=pl.BlockSpec((1, tm, d), lambda i, j: (i, j, 0)),
        out_shape=jax.ShapeDtypeStruct((b, t, d), F32),
        scratch_shapes=[pltpu.VMEM((2 * D_FF // LANES, HALO + tm, LANES), F32), pltpu.VMEM((tm, D_FF), BF16)],
        compiler_params=pltpu.CompilerParams(
            dimension_semantics=("arbitrary", "arbitrary"), vmem_limit_bytes=VMEM_LIMIT),
        name="ffn",
    )(x, pre_w, w_up, conv_w, conv_b, w_down, post_w)


def kernel(x, mem, mix_pre_norm, w_in, attn_sinks, hgrn_lb_logits, hgrn_out_norm, w_out, mix_post_norm, ca_pre_norm,
           mem_norm, ca_wq, ca_wk, ca_wv, ca_wo, ca_post_norm, ffn_pre_norm, ffn_w_up, ffn_conv_w, ffn_conv_b,
           ffn_w_down, ffn_post_norm):
    depth = w_in.shape[0]
    assert depth == 1 and hgrn_lb_logits.shape[0] == 2
    bf = lambda w: w.astype(BF16)
    for l in range(depth):
        later = (ca_wq[l], ca_wk[l], ca_wv[l], ca_wo[l], ffn_w_up[l])
        x, (wq, wk, wv, wo, w_up) = _mixer(
            x, mix_pre_norm[l:l + 1], bf(w_in[l]), attn_sinks[l], hgrn_lb_logits, hgrn_out_norm[l:l + 1],
            bf(w_out[l]), mix_post_norm[l:l + 1], later)
        k, v = _mem_kv(mem, mem_norm[l:l + 1], wk, wv)
        x = _xattn(x, ca_pre_norm[l:l + 1], wq, k, v, wo, ca_post_norm[l:l + 1])
        x = _ffn(x, ffn_pre_norm[l:l + 1], w_up, ffn_conv_w[l], ffn_conv_b[l:l + 1], bf(ffn_w_down[l]),
                 ffn_post_norm[l:l + 1])
    return x
```

```python
import functools

import jax
import jax.numpy as jnp
from jax import lax
from jax.experimental import pallas as pl
from jax.experimental.pallas import tpu as pltpu

F32 = jnp.float32
BF16 = jnp.bfloat16

EPS = 1e-6
LOG2E = 1.4426950408889634
D_MODEL = 1024
LANES = 128
BF16_SUBLANES = 16
ATTN_WIDTH = 512
ATTN_HEAD_DIM = 64
ATTN_GROUP = 4
ATTN_KV_HEADS = 2
ATTN_KV_WIDTH = ATTN_KV_HEADS * ATTN_HEAD_DIM
ATTN_BLOCK = 128
HGRN_HEADS = 4
HGRN_DIM = 128
HGRN_WIDTH = HGRN_HEADS * HGRN_DIM
HGRN_PAIR = 2 * HGRN_DIM
COL_Q = 0
COL_K = COL_Q + ATTN_WIDTH
COL_V = COL_K + ATTN_KV_WIDTH
COL_HQ = COL_V + ATTN_KV_WIDTH
COL_HF = COL_HQ + HGRN_WIDTH
COL_HI = COL_HF + HGRN_WIDTH
COL_HG = COL_HI + HGRN_WIDTH
IN_PROJ_WIDTH = COL_HG + HGRN_WIDTH
IN_PROJ_STEP = 512
OUT_PROJ_STEP = 512
HGRN_CHUNK = 64
HGRN_SUB = 16
HGRN_NSUB = HGRN_CHUNK // HGRN_SUB
CA_HEADS = 4
CA_HEAD_DIM = 256
D_FF = 2816
FF_CHUNK = 256
FF_DOWN_GROUP = 6
HALO = 8

TM_MIX = 512
TS_MIX = 256
TM_CA = 1024
TM_FFN = 512

VMEM_LIMIT = 56 * 1024 * 1024

NT_DIMS = (((1,), (1,)), ((), ()))


def _rms(x, w):
    return x * lax.rsqrt(jnp.mean(x * x, axis=-1, keepdims=True) + EPS) * w


def _dot(a, b):
    return jnp.dot(a, b, preferred_element_type=F32)


def _dot_nt(a, b):
    return lax.dot_general(a, b, NT_DIMS, preferred_element_type=F32)


def _mem_kv_kernel(mem_ref, nw_ref, wk_ref, wv_ref, k_ref, v_ref):
    mn = _rms(mem_ref[0], nw_ref[...]).astype(BF16)
    k_ref[0] = _dot(mn, wk_ref[...]).astype(BF16)
    v_ref[0] = _dot(mn, wv_ref[...]).astype(BF16)


def _mem_kv(mem, norm_w, wk, wv):
    b, m, d = mem.shape
    full = lambda shape: pl.BlockSpec(shape, lambda i: (0,) * len(shape))
    return pl.pallas_call(
        _mem_kv_kernel,
        grid=(b,),
        in_specs=[pl.BlockSpec((1, m, d), lambda i: (i, 0, 0)), full((1, d)), full(wk.shape), full(wv.shape)],
        out_specs=[pl.BlockSpec((1, m, d), lambda i: (i, 0, 0))] * 2,
        out_shape=[jax.ShapeDtypeStruct((b, m, d), BF16)] * 2,
        compiler_params=pltpu.CompilerParams(dimension_semantics=("arbitrary",), vmem_limit_bytes=VMEM_LIMIT),
        name="mem_kv",
    )(mem, norm_w, wk, wv)


def _swa_scores(z_ref, kv_ref, r_ref, row0, blk, j):
    nq = ATTN_BLOCK
    gw = ATTN_GROUP * ATTN_HEAD_DIM
    r0 = blk * nq
    q = z_ref[r0:r0 + nq, COL_Q + j * gw:COL_Q + (j + 1) * gw] * (ATTN_HEAD_DIM ** -0.5 * LOG2E)
    kk = kv_ref[r0:r0 + 2 * nq, 0:ATTN_KV_WIDTH]
    k_sw = pltpu.roll(kk, ATTN_HEAD_DIM, axis=1)
    low = lax.broadcasted_iota(jnp.int32, kk.shape, 1) < ATTN_HEAD_DIM
    kj = jnp.where(low, kk, k_sw) if j == 0 else jnp.where(low, k_sw, kk)
    k_rep = jnp.concatenate([kj, kj], axis=1).astype(BF16)
    head_of_lane = lax.broadcasted_iota(jnp.int32, (nq, gw), 1) // ATTN_HEAD_DIM
    qs = jnp.concatenate([jnp.where(head_of_lane == h, q, 0.0) for h in range(ATTN_GROUP)], axis=0).astype(BF16)
    r_ref[row0:row0 + ATTN_GROUP * nq, :] = _dot_nt(qs, k_rep)


def _swa_finish(r_ref, s_row0, o_row0, kv_ref, sink_ref, mix_ref, blk, j, first_block):
    nq = ATTN_BLOCK
    s = r_ref[s_row0:s_row0 + ATTN_GROUP * nq, :]
    gw = ATTN_GROUP * ATTN_HEAD_DIM
    r0 = blk * nq
    qi = lax.broadcasted_iota(jnp.int32, (nq, 2 * nq), 0)
    kj = lax.broadcasted_iota(jnp.int32, (nq, 2 * nq), 1)
    diff = qi + nq - kj
    allowed = (diff >= 0) & (diff < nq) & ((kj >= nq) | jnp.logical_not(first_block))
    ps, denoms = [], []
    for h in range(ATTN_GROUP):
        sink = sink_ref[j * ATTN_GROUP + h] * LOG2E
        sh = jnp.where(allowed, s[h * nq:(h + 1) * nq], -jnp.inf)
        mh = jnp.maximum(jnp.max(sh, axis=-1, keepdims=True), sink)
        ph = jnp.exp2(sh - mh)
        ps.append(ph.astype(BF16))
        denoms.append(jnp.sum(ph, axis=-1, keepdims=True) + jnp.exp2(sink - mh))
    p = jnp.concatenate(ps, axis=0)

    vv = kv_ref[r0:r0 + 2 * nq, ATTN_KV_WIDTH:2 * ATTN_KV_WIDTH]
    v_sw = pltpu.roll(vv, ATTN_HEAD_DIM, axis=1)
    low = lax.broadcasted_iota(jnp.int32, vv.shape, 1) < ATTN_HEAD_DIM
    vj = jnp.where(low, vv, v_sw) if j == 0 else jnp.where(low, v_sw, vv)
    v_rep = jnp.concatenate([vj, vj], axis=1).astype(BF16)
    r_ref[o_row0:o_row0 + ATTN_GROUP * nq, :] = _dot(p, v_rep)
    head_of_lane = lax.broadcasted_iota(jnp.int32, (nq, gw), 1) // ATTN_HEAD_DIM
    og = None
    for h in range(ATTN_GROUP):
        oh = r_ref[o_row0 + h * nq:o_row0 + (h + 1) * nq, :] * (1.0 / denoms[h])
        oh = jnp.where(head_of_lane == h, oh, 0.0)
        og = oh if og is None else og + oh
    mix_ref[r0:r0 + nq, j * gw:(j + 1) * gw] = og.astype(BF16)


def _hgrn_gates(z_ref, lb, q_ref, k_ref, ghl_ref, rows):
    hq = z_ref[rows, COL_HQ:COL_HQ + HGRN_WIDTH] * (0.5 * HGRN_DIM ** -0.5)
    q_ref[rows, :] = hq + hq * jnp.tanh(z_ref[rows, COL_HQ:COL_HQ + HGRN_WIDTH] * 0.5)
    t = jnp.tanh(z_ref[rows, COL_HF:COL_HF + HGRN_WIDTH] * 0.5)
    half_span = 0.5 * (1.0 - lb)
    f = (1.0 - half_span) + half_span * t
    k_ref[rows, :] = half_span - half_span * t
    g = jnp.log(f) * LOG2E
    g_hi = g.astype(BF16)
    ghl_ref[rows, 0:HGRN_WIDTH] = g_hi
    ghl_ref[rows, HGRN_WIDTH:2 * HGRN_WIDTH] = (g - g_hi.astype(F32)).astype(BF16)


def _hgrn_cumsum(ghl_ref, cs_ref, bc_ref):
    tm = bc_ref.shape[0]
    grp = min(tm, 2 * LANES)
    ti = lax.broadcasted_iota(jnp.int32, (grp, grp), 0)
    si = lax.broadcasted_iota(jnp.int32, (grp, grp), 1)
    same_chunk = jnp.bitwise_xor(ti, si) < HGRN_CHUNK
    tri = ((si <= ti) & same_chunk).astype(BF16)
    for r0 in range(0, tm, grp):
        cs_ref[r0:r0 + grp, :] = _dot(tri, ghl_ref[r0:r0 + grp, :])
    bc_ref[...] = cs_ref[:, 0:HGRN_WIDTH] + cs_ref[:, HGRN_WIDTH:2 * HGRN_WIDTH]


def _hgrn_factors(c, q_ref, k_ref, bc_ref, qe_ref, kecat_ref, qd_ref, kd_ref):
    r0c = c * HGRN_CHUNK
    rows = slice(r0c, r0c + HGRN_CHUNK)
    bc = bc_ref[rows, :]
    q = q_ref[rows, :]
    k = k_ref[rows, :]
    for i in range(HGRN_NSUB):
        r0 = i * HGRN_SUB
        anchor = bc[r0 + HGRN_SUB // 2 - 1:r0 + HGRN_SUB // 2, :]
        qe_ref[r0c + r0:r0c + r0 + HGRN_SUB, :] = (q[r0:r0 + HGRN_SUB] * jnp.exp2(bc[r0:r0 + HGRN_SUB] - anchor)
                                                   ).astype(BF16)
        n_valid = r0 + HGRN_SUB
        ke = k[0:n_valid] * jnp.exp2(anchor - bc[0:n_valid])
        kecat_ref[c, i * HGRN_CHUNK:i * HGRN_CHUNK + n_valid, :] = ke.astype(BF16)
    qd_ref[rows, :] = (q * jnp.exp2(bc)).astype(BF16)
    b_last = bc[HGRN_CHUNK - 1:HGRN_CHUNK, :]
    e = c % 2
    kd_ref[c, e * HGRN_CHUNK:(e + 1) * HGRN_CHUNK, :] = (k * jnp.exp2(b_last - bc)).astype(BF16)
    kd_ref[c, (1 - e) * HGRN_CHUNK:(2 - e) * HGRN_CHUNK, :] = jnp.zeros((HGRN_CHUNK, HGRN_WIDTH), BF16)
    return jnp.exp2(b_last)


def _hgrn_scores(c, qe_ref, kecat_ref, r_ref, row0):
    rows = slice(c * HGRN_CHUNK, (c + 1) * HGRN_CHUNK)
    for h in range(HGRN_HEADS):
        hs = slice(h * HGRN_DIM, (h + 1) * HGRN_DIM)
        r_ref[row0 + h * HGRN_CHUNK:row0 + (h + 1) * HGRN_CHUNK, :] = _dot_nt(qe_ref[rows, hs], kecat_ref[c, :, hs])


def _hgrn_scores_finish(c, r_ref, row0, a_ref):
    rows = slice(c * HGRN_CHUNK, (c + 1) * HGRN_CHUNK)
    t = lax.broadcasted_iota(jnp.int32, (HGRN_CHUNK, LANES), 0)
    l = lax.broadcasted_iota(jnp.int32, (HGRN_CHUNK, LANES), 1)
    keep = ((l // HGRN_CHUNK) == ((t // HGRN_SUB) % 2)) & ((l % HGRN_CHUNK) <= t)
    half = HGRN_CHUNK // 2
    for h in range(HGRN_HEADS):
        hs = slice(h * HGRN_DIM, (h + 1) * HGRN_DIM)
        pr = row0 + h * HGRN_CHUNK
        a = jnp.concatenate([r_ref[pr:pr + half, 0:LANES], r_ref[pr + half:pr + HGRN_CHUNK, LANES:2 * LANES]], axis=0)
        a_ref[rows, hs] = jnp.where(keep, a, 0.0).astype(BF16)


def _pair_diag_mask():
    r = lax.broadcasted_iota(jnp.int32, (HGRN_PAIR, HGRN_PAIR), 0)
    l = lax.broadcasted_iota(jnp.int32, (HGRN_PAIR, HGRN_PAIR), 1)
    return (r // HGRN_DIM) == (l // HGRN_DIM)


def _hgrn_transpose_v(z_ref, vt_ref):
    eye = (lax.broadcasted_iota(jnp.int32, (LANES, LANES), 0)
           == lax.broadcasted_iota(jnp.int32, (LANES, LANES), 1)).astype(BF16)
    for a in range(HGRN_WIDTH // LANES):
        v = z_ref[:, COL_HI + a * LANES:COL_HI + (a + 1) * LANES].astype(BF16)
        vt_ref[a * LANES:(a + 1) * LANES, :] = _dot_nt(eye, v).astype(BF16)


def _hgrn_states(n_chunks, vt_ref, kd_ref, st_ref, stbf_ref, decays, r_ref, row0):
    for p in range(HGRN_HEADS // 2):
        ps = slice(p * HGRN_PAIR, (p + 1) * HGRN_PAIR)
        for c in range(n_chunks):
            tok = slice((c // 2) * LANES, (c // 2 + 1) * LANES)
            ur = row0 + (p * n_chunks + c) * HGRN_PAIR
            r_ref[ur:ur + HGRN_PAIR, :] = _dot(vt_ref[ps, tok], kd_ref[c, :, ps])
    for h in range(HGRN_HEADS):
        p, blk = h // 2, slice((h % 2) * HGRN_DIM, (h % 2 + 1) * HGRN_DIM)
        st = st_ref[h]
        for c in range(n_chunks):
            ur = row0 + (p * n_chunks + c) * HGRN_PAIR + (h % 2) * HGRN_DIM
            stbf_ref[c, p, blk, blk] = st.astype(BF16)
            st = st * decays[c][:, h * HGRN_DIM:(h + 1) * HGRN_DIM] + r_ref[ur:ur + HGRN_DIM, blk]
        st_ref[h] = st


def _hgrn_outputs(c, z_ref, a_ref, qd_ref, stbf_ref, onw, mix_ref, r_ref, row0):
    rows = slice(c * HGRN_CHUNK, (c + 1) * HGRN_CHUNK)
    diag = _pair_diag_mask()
    for p in range(HGRN_HEADS // 2):
        ps = slice(p * HGRN_PAIR, (p + 1) * HGRN_PAIR)
        vp = z_ref[rows, COL_HI + p * HGRN_PAIR:COL_HI + (p + 1) * HGRN_PAIR].astype(BF16)
        v_bd = jnp.where(diag, jnp.concatenate([vp] * (HGRN_PAIR // HGRN_CHUNK), axis=0), 0.0)
        orow = row0 + p * HGRN_CHUNK
        r_ref[orow:orow + HGRN_CHUNK, :] = _dot(a_ref[rows, ps], v_bd) + _dot_nt(qd_ref[rows, ps], stbf_ref[c, p])
        o = r_ref[orow:orow + HGRN_CHUNK, :]
        gr = z_ref[rows, COL_HG + p * HGRN_PAIR:COL_HG + (p + 1) * HGRN_PAIR]
        hg = 0.5 * gr
        gate = hg + hg * jnp.tanh(hg)
        outs = []
        for hh in range(2):
            oh = o[:, hh * HGRN_DIM:(hh + 1) * HGRN_DIM]
            outs.append(oh * lax.rsqrt(jnp.mean(oh * oh, axis=-1, keepdims=True) + EPS) * onw)
        rec = jnp.concatenate(outs, axis=1) * gate
        mix_ref[rows, ATTN_WIDTH + p * HGRN_PAIR:ATTN_WIDTH + (p + 1) * HGRN_PAIR] = rec.astype(BF16)


def _mixer_kernel(n_cast, x_ref, prew_ref, win_ref, sink_ref, lbl_ref, onw_ref, wout_ref, postw_ref, *rest):
    cast_in, o_ref, cast_out = rest[:n_cast], rest[n_cast], rest[n_cast + 1:2 * n_cast + 1]
    (z_ref, kv_ref, st_ref, mix_ref, q_ref, k_ref, ghl_ref, bc_ref, qe_ref, kecat_ref, qd_ref, kd_ref,
     a_ref, vt_ref, stbf_ref, cs_ref, r_ref, h_ref, m_ref) = rest[2 * n_cast + 1:]
    for src, dst in zip(cast_in, cast_out):
        dst[...] = src[...].astype(BF16)
    n = pl.program_id(1)
    tm = x_ref.shape[1]
    ts = z_ref.shape[1]
    n_sub = tm // ts
    n_chunks = ts // HGRN_CHUNK
    units = [(blk, j) for blk in range(ts // ATTN_BLOCK) for j in range(ATTN_KV_HEADS)]

    @pl.when(n == 0)
    def _():
        kv_ref[0:ATTN_BLOCK, :] = jnp.zeros((ATTN_BLOCK, 2 * ATTN_KV_WIDTH), F32)
        st_ref[...] = jnp.zeros_like(st_ref)
        kecat_ref[...] = jnp.zeros_like(kecat_ref)
        stbf_ref[...] = jnp.zeros_like(stbf_ref)

    l0 = lbl_ref[0:1, :]
    l1 = lbl_ref[1:2, :]
    lm = jnp.maximum(l0, l1)
    e0 = jnp.exp(l0 - lm)
    lb = e0 / (e0 + jnp.exp(l1 - lm))

    unit_rows = ATTN_GROUP * ATTN_BLOCK
    s_base = 0
    pv_base = s_base + len(units) * unit_rows
    p_base = pv_base + len(units) * unit_rows
    upd_base = p_base + n_chunks * HGRN_HEADS * HGRN_CHUNK
    o_base = upd_base + n_chunks * HGRN_HEADS // 2 * HGRN_PAIR
    assert o_base + n_chunks * HGRN_HEADS // 2 * HGRN_CHUNK == r_ref.shape[0]

    def p_rows(c):
        return p_base + c * HGRN_HEADS * HGRN_CHUNK

    def o_rows(c):
        return o_base + c * HGRN_HEADS // 2 * HGRN_CHUNK

    def project_in(s):
        rows = slice(s * ts, (s + 1) * ts)
        z = z_ref.at[s % 2]

        def norm():
            h_ref[s % 2] = _rms(x_ref[0, rows, :], prew_ref[...]).astype(BF16)

        def cols(c0, c1):
            def step():
                z[:, c0:c1] = _dot(h_ref[s % 2], win_ref[:, c0:c1])
            return step

        def keep_kv():
            kv_ref[ATTN_BLOCK + s * ts:ATTN_BLOCK + (s + 1) * ts, :] = z[:, COL_K:COL_HQ]

        bounds = list(range(COL_HQ, IN_PROJ_WIDTH, IN_PROJ_STEP)) + [IN_PROJ_WIDTH]
        hgrn_steps = [norm] + [cols(a, b) for a, b in zip(bounds[:-1], bounds[1:])]
        bounds = list(range(0, COL_HQ, IN_PROJ_STEP)) + [COL_HQ]
        return hgrn_steps, [cols(a, b) for a, b in zip(bounds[:-1], bounds[1:])] + [keep_kv]

    def project_out(s):
        rows = slice(s * ts, (s + 1) * ts)

        def cols(c0, c1):
            def step():
                m_ref[s % 2, :, c0:c1] = _dot(mix_ref[s % 2], wout_ref[:, c0:c1])
            return step

        def finish():
            o_ref[0, rows, :] = x_ref[0, rows, :] + _rms(m_ref[s % 2], postw_ref[...])

        return [cols(c0, c0 + OUT_PROJ_STEP) for c0 in range(0, D_MODEL, OUT_PROJ_STEP)] + [finish]

    def mix_tokens(s, fill, fill_own):
        z, mix = z_ref.at[s % 2], mix_ref.at[s % 2]
        q, k, ghl, bc, qe, kecat, qd, kd = q_ref, k_ref, ghl_ref, bc_ref, qe_ref, kecat_ref, qd_ref, kd_ref
        a, vt, stbf, cs, r = a_ref, vt_ref, stbf_ref, cs_ref, r_ref
        kv = kv_ref.at[pl.ds(s * ts, ATTN_BLOCK + ts)]

        def first_block(blk):
            return (n == 0) if (s == 0 and blk == 0) else jnp.bool_(False)

        fill(2)
        for c in range(n_chunks):
            _hgrn_gates(z, lb, q, k, ghl, slice(c * HGRN_CHUNK, (c + 1) * HGRN_CHUNK))
            if c % 2 == 1:
                fill()
        _hgrn_transpose_v(z, vt)
        _hgrn_cumsum(ghl, cs, bc)
        fill_own()
        _swa_scores(z, kv, r, s_base, *units[0])
        decays = []
        for c in range(n_chunks):
            fill()
            decays.append(_hgrn_factors(c, q, k, bc, qe, kecat, qd, kd))
            _hgrn_scores(c, qe, kecat, r, p_rows(c))
        fill()
        _hgrn_states(n_chunks, vt, kd, st_ref, stbf, decays, r, upd_base)
        for c in range(n_chunks):
            _hgrn_scores_finish(c, r, p_rows(c), a)
        for u, (blk, j) in enumerate(units):
            fill()
            if u + 1 < len(units):
                _swa_scores(z, kv, r, s_base + (u + 1) * unit_rows, *units[u + 1])
            if u < n_chunks:
                _hgrn_outputs(u, z, a, qd, stbf, onw_ref[...], mix, r, o_rows(u))
            _swa_finish(r, s_base + u * unit_rows, pv_base + u * unit_rows, kv, sink_ref, mix, blk, j,
                        first_block(blk))
        for c in range(len(units), n_chunks):
            _hgrn_outputs(c, z, a, qd, stbf, onw_ref[...], mix, r, o_rows(c))
        fill(None)

    hgrn_steps, own = project_in(0)
    for step in hgrn_steps:
        step()
    for s in range(n_sub):
        nxt, nxt_own = project_in(s + 1) if s + 1 < n_sub else ([], [])
        prv = project_out(s - 1) if s > 0 else []
        pending = list(own)
        while nxt or prv:
            if nxt:
                pending.append(nxt.pop(0))
            if prv:
                pending.append(prv.pop(0))
        own_left = [len(own)]

        def fill(count=1):
            for _ in range(len(pending) if count is None else min(count, len(pending))):
                pending.pop(0)()
                own_left[0] -= 1

        def fill_own():
            fill(max(own_left[0], 0))

        mix_tokens(s, fill, fill_own)
        own = nxt_own
    for step in project_out(n_sub - 1):
        step()
    kv_ref[0:ATTN_BLOCK, :] = kv_ref[tm:tm + ATTN_BLOCK, :]


def _mixer_result_rows(tm):
    units = (tm // ATTN_BLOCK) * ATTN_KV_HEADS
    chunks = tm // HGRN_CHUNK
    return (2 * units * ATTN_GROUP * ATTN_BLOCK + chunks * HGRN_HEADS * HGRN_CHUNK
            + chunks * (HGRN_HEADS // 2) * (HGRN_PAIR + HGRN_CHUNK))


def _cast_block_spec(w, b, nt):
    rows, cols = w.shape
    assert rows % (b * nt * BF16_SUBLANES) == 0
    return pl.BlockSpec((rows // (b * nt), cols), lambda i, j: (i * nt + j, 0))


def _mixer(x, pre_w, w_in, sinks, lb_logits, out_norm_w, w_out, post_w, later_weights):
    b, t, d = x.shape
    tm, ts = TM_MIX, TS_MIX
    ns = tm // ts
    nt = t // tm
    assert t % tm == 0 and d == D_MODEL and w_in.shape == (D_MODEL, IN_PROJ_WIDTH)
    assert tm % ts == 0 and ts % (2 * LANES) == 0
    full = lambda shape: pl.BlockSpec(shape, lambda i, j: (0,) * len(shape), pipeline_mode=pl.Buffered(1))
    cast_specs = [_cast_block_spec(w, b, nt) for w in later_weights]
    outs = pl.pallas_call(
        functools.partial(_mixer_kernel, len(later_weights)),
        grid=(b, nt),
        in_specs=[
            pl.BlockSpec((1, tm, d), lambda i, j: (i, j, 0)),
            full((1, d)),
            full(w_in.shape),
            pl.BlockSpec(memory_space=pltpu.SMEM),
            full(lb_logits.shape),
            full((1, HGRN_DIM)),
            full(w_out.shape),
            full((1, d)),
        ] + cast_specs,
        out_specs=[pl.BlockSpec((1, tm, d), lambda i, j: (i, j, 0))] + cast_specs,
        out_shape=[jax.ShapeDtypeStruct((b, t, d), F32)] + [jax.ShapeDtypeStruct(w.shape, BF16) for w in later_weights],
        scratch_shapes=[
            pltpu.VMEM((2, ts, IN_PROJ_WIDTH), F32),
            pltpu.VMEM((ATTN_BLOCK + tm, 2 * ATTN_KV_WIDTH), F32),
            pltpu.VMEM((HGRN_HEADS, HGRN_DIM, HGRN_DIM), F32),
            pltpu.VMEM((2, ts, ATTN_WIDTH + HGRN_WIDTH), BF16),
            pltpu.VMEM((ts, HGRN_WIDTH), F32),
            pltpu.VMEM((ts, HGRN_WIDTH), F32),
            pltpu.VMEM((ts, 2 * HGRN_WIDTH), BF16),
            pltpu.VMEM((ts, HGRN_WIDTH), F32),
            pltpu.VMEM((ts, HGRN_WIDTH), BF16),
            pltpu.VMEM((ts // HGRN_CHUNK, HGRN_NSUB * HGRN_CHUNK, HGRN_WIDTH), BF16),
            pltpu.VMEM((ts, HGRN_WIDTH), BF16),
            pltpu.VMEM((ts // HGRN_CHUNK, 2 * HGRN_CHUNK, HGRN_WIDTH), BF16),
            pltpu.VMEM((ts, HGRN_WIDTH), BF16),
            pltpu.VMEM((HGRN_WIDTH, ts), BF16),
            pltpu.VMEM((ts // HGRN_CHUNK, HGRN_HEADS // 2, HGRN_PAIR, HGRN_PAIR), BF16),
            pltpu.VMEM((ts, 2 * HGRN_WIDTH), F32),
            pltpu.VMEM((_mixer_result_rows(ts), HGRN_PAIR), F32),
            pltpu.VMEM((2, ts, D_MODEL), BF16),
            pltpu.VMEM((2, ts, D_MODEL), F32),
        ],
        compiler_params=pltpu.CompilerParams(
            dimension_semantics=("arbitrary", "arbitrary"), vmem_limit_bytes=VMEM_LIMIT),
        name="mixer",
    )(x, pre_w, w_in, sinks, lb_logits, out_norm_w, w_out, post_w, *later_weights)
    return outs[0], outs[1:]


def _xattn_kernel(x_ref, prew_ref, wq_ref, k_ref, v_ref, wo_ref, postw_ref, o_ref, cat_ref):
    x = x_ref[0]
    h = _rms(x, prew_ref[...]).astype(BF16)
    q = (_dot(h, wq_ref[...]) * (CA_HEAD_DIM ** -0.5 * LOG2E)).astype(BF16)

    def scores(hd):
        hs = slice(hd * CA_HEAD_DIM, (hd + 1) * CA_HEAD_DIM)
        return _dot_nt(q[:, hs], k_ref[0, :, hs])

    s_next = scores(0)
    for hd in range(CA_HEADS):
        hs = slice(hd * CA_HEAD_DIM, (hd + 1) * CA_HEAD_DIM)
        s = s_next
        if hd + 1 < CA_HEADS:
            s_next = scores(hd + 1)
        m = jnp.max(s, axis=-1, keepdims=True)
        p = jnp.exp2(s - m)
        p = (p * (1.0 / jnp.sum(p, axis=-1, keepdims=True))).astype(BF16)
        cat_ref[:, hs] = _dot(p, v_ref[0, :, hs]).astype(BF16)
    half = x.shape[0] // 2
    for rows in (slice(0, half), slice(half, 2 * half)):
        c = _dot(cat_ref[rows, :], wo_ref[...])
        o_ref[0, rows, :] = x[rows] + _rms(c, postw_ref[...])


def _xattn(x, pre_w, wq, k, v, wo, post_w):
    b, t, d = x.shape
    tm = TM_CA
    nm = k.shape[1]
    assert t % tm == 0
    full = lambda shape: pl.BlockSpec(shape, lambda i, j: (0,) * len(shape), pipeline_mode=pl.Buffered(1))
    return pl.pallas_call(
        _xattn_kernel,
        grid=(b, t // tm),
        in_specs=[
            pl.BlockSpec((1, tm, d), lambda i, j: (i, j, 0)),
            full((1, d)),
            full(wq.shape),
            pl.BlockSpec((1, nm, d), lambda i, j: (i, 0, 0)),
            pl.BlockSpec((1, nm, d), lambda i, j: (i, 0, 0)),
            full(wo.shape),
            full((1, d)),
        ],
        out_specs=pl.BlockSpec((1, tm, d), lambda i, j: (i, j, 0)),
        out_shape=jax.ShapeDtypeStruct((b, t, d), F32),
        scratch_shapes=[pltpu.VMEM((tm, d), BF16)],
        compiler_params=pltpu.CompilerParams(
            dimension_semantics=("arbitrary", "arbitrary"), vmem_limit_bytes=VMEM_LIMIT),
        name="xattn",
    )(x, pre_w, wq, k, v, wo, post_w)


def _gelu_tanh(x):
    return 0.5 * x * (1.0 + jnp.tanh(0.7978845608028654 * (x + 0.044715 * (x * x * x))))


def _causal_conv3(u_ref, kb, tm, cw_ref, cb_ref):
    ls = slice(kb * LANES, (kb + 1) * LANES)
    out = (cw_ref[2:3, ls] * u_ref[kb, HALO:HALO + tm, :] + cw_ref[1:2, ls] * u_ref[kb, HALO - 1:HALO - 1 + tm, :]
           + cw_ref[0:1, ls] * u_ref[kb, HALO - 2:HALO - 2 + tm, :]) + cb_ref[:, ls]
    u_ref[kb, 0:HALO, :] = u_ref[kb, tm:tm + HALO, :]
    return out


def _ffn_kernel(x_ref, prew_ref, wup_ref, cw_ref, cb_ref, wdn_ref, postw_ref, o_ref, u_ref, act_ref):
    n = pl.program_id(1)
    tm = x_ref.shape[1]
    n_chunks = D_FF // FF_CHUNK
    kb_per_chunk = FF_CHUNK // LANES
    kb_val = D_FF // LANES

    @pl.when(n == 0)
    def _():
        u_ref[:, 0:HALO, :] = jnp.zeros((u_ref.shape[0], HALO, LANES), F32)

    x = x_ref[0]
    h = _rms(x, prew_ref[...]).astype(BF16)

    def up(c):
        for base, col0 in ((0, 0), (kb_val, D_FF)):
            u = _dot(h, wup_ref[:, col0 + c * FF_CHUNK:col0 + (c + 1) * FF_CHUNK])
            for j in range(kb_per_chunk):
                u_ref[base + c * kb_per_chunk + j, HALO:HALO + tm, :] = u[:, j * LANES:(j + 1) * LANES]

    acc = None
    group_start = 0
    up(0)
    for c in range(n_chunks):
        if c + 1 < n_chunks:
            up(c + 1)
        for j in range(kb_per_chunk):
            kb = c * kb_per_chunk + j
            gate = _causal_conv3(u_ref, kb, tm, cw_ref, cb_ref)
            val = _causal_conv3(u_ref, kb_val + kb, tm, cw_ref, cb_ref)
            act_ref[:, kb * LANES:(kb + 1) * LANES] = (_gelu_tanh(gate) * val).astype(BF16)
        if (c + 1) % FF_DOWN_GROUP == 0 or c + 1 == n_chunks:
            ks = slice(group_start * FF_CHUNK, (c + 1) * FF_CHUNK)
            part = _dot(act_ref[:, ks], wdn_ref[ks, :])
            acc = part if acc is None else acc + part
            group_start = c + 1
    o_ref[0] = x + _rms(acc, postw_ref[...])


def _ffn(x, pre_w, w_up, conv_w, conv_b, w_down, post_w):
    b, t, d = x.shape
    tm = TM_FFN
    assert t % tm == 0 and w_up.shape == (D_MODEL, 2 * D_FF) and D_FF % FF_CHUNK == 0
    full = lambda shape: pl.BlockSpec(shape, lambda i, j: (0,) * len(shape), pipeline_mode=pl.Buffered(1))
    return pl.pallas_call(
        _ffn_kernel,
        grid=(b, t // tm),
        in_specs=[
            pl.BlockSpec((1, tm, d), lambda i, j: (i, j, 0)),
            full((1, d)),
            full(w_up.shape),
            full(conv_w.shape),
            full((1, 2 * D_FF)),
            full(w_down.shape),
            full((1, d)),
        ],
        out_specs=pl.BlockSpec((1, tm, d), lambda i, j: (i, j, 0)),
        out_shape=jax.ShapeDtypeStruct((b, t, d), F32),
        scratch_shapes=[pltpu.VMEM((2 * D_FF // LANES, HALO + tm, LANES), F32), pltpu.VMEM((tm, D_FF), BF16)],
        compiler_params=pltpu.CompilerParams(
            dimension_semantics=("arbitrary", "arbitrary"), vmem_limit_bytes=VMEM_LIMIT),
        name="ffn",
    )(x, pre_w, w_up, conv_w, conv_b, w_down, post_w)


def _xattn_ffn_kernel(x_ref, ca_prew_ref, wq_ref, k_ref, v_ref, wo_ref, ca_postw_ref, ffn_prew_ref, wup_ref, cw_ref,
                      cb_ref, wdn_ref, ffn_postw_ref, o_ref, cat_ref, x2_ref, u_ref, act_ref):
    _xattn_kernel(x_ref, ca_prew_ref, wq_ref, k_ref, v_ref, wo_ref, ca_postw_ref, x2_ref, cat_ref)
    _ffn_kernel(x2_ref, ffn_prew_ref, wup_ref, cw_ref, cb_ref, wdn_ref, ffn_postw_ref, o_ref, u_ref, act_ref)


def _xattn_ffn(x, ca_pre_w, wq, k, v, wo, ca_post_w, ffn_pre_w, w_up, conv_w, conv_b, w_down, ffn_post_w):
    b, t, d = x.shape
    tm = TM_FFN
    nm = k.shape[1]
    assert t % tm == 0 and w_up.shape == (D_MODEL, 2 * D_FF) and D_FF % FF_CHUNK == 0
    full = lambda shape: pl.BlockSpec(shape, lambda i, j: (0,) * len(shape), pipeline_mode=pl.Buffered(1))
    tile = pl.BlockSpec((1, tm, d), lambda i, j: (i, j, 0))
    mem_blk = pl.BlockSpec((1, nm, d), lambda i, j: (i, 0, 0))
    return pl.pallas_call(
        _xattn_ffn_kernel,
        grid=(b, t // tm),
        in_specs=[tile, full((1, d)), full(wq.shape), mem_blk, mem_blk, full(wo.shape), full((1, d)),
                  full((1, d)), full(w_up.shape), full(conv_w.shape), full((1, 2 * D_FF)), full(w_down.shape),
                  full((1, d))],
        out_specs=tile,
        out_shape=jax.ShapeDtypeStruct((b, t, d), F32),
        scratch_shapes=[
            pltpu.VMEM((tm, d), BF16),
            pltpu.VMEM((1, tm, d), F32),
            pltpu.VMEM((2 * D_FF // LANES, HALO + tm, LANES), F32),
            pltpu.VMEM((tm, D_FF), BF16),
        ],
        compiler_params=pltpu.CompilerParams(
            dimension_semantics=("arbitrary", "arbitrary"), vmem_limit_bytes=VMEM_LIMIT),
        name="xattn_ffn",
    )(x, ca_pre_w, wq, k, v, wo, ca_post_w, ffn_pre_w, w_up, conv_w, conv_b, w_down, ffn_post_w)


def kernel(x, mem, mix_pre_norm, w_in, attn_sinks, hgrn_lb_logits, hgrn_out_norm, w_out, mix_post_norm, ca_pre_norm,
           mem_norm, ca_wq, ca_wk, ca_wv, ca_wo, ca_post_norm, ffn_pre_norm, ffn_w_up, ffn_conv_w, ffn_conv_b,
           ffn_w_down, ffn_post_norm):
    depth = w_in.shape[0]
    assert depth == 1 and hgrn_lb_logits.shape[0] == 2
    bf = lambda w: w.astype(BF16)
    for l in range(depth):
        later = (ca_wq[l], ca_wk[l], ca_wv[l], ca_wo[l], ffn_w_up[l])
        x, (wq, wk, wv, wo, w_up) = _mixer(
            x, mix_pre_norm[l:l + 1], bf(w_in[l]), attn_sinks[l], hgrn_lb_logits, hgrn_out_norm[l:l + 1],
            bf(w_out[l]), mix_post_norm[l:l + 1], later)
        k, v = _mem_kv(mem, mem_norm[l:l + 1], wk, wv)
        x = _xattn_ffn(x, ca_pre_norm[l:l + 1], wq, k, v, wo, ca_post_norm[l:l + 1], ffn_pre_norm[l:l + 1], w_up,
                       ffn_conv_w[l], ffn_conv_b[l:l + 1], bf(ffn_w_down[l]), ffn_post_norm[l:l + 1])
    return x
```

```python
import functools

import jax
import jax.numpy as jnp
from jax import lax
from jax.experimental import pallas as pl
from jax.experimental.pallas import tpu as pltpu

F32 = jnp.float32
BF16 = jnp.bfloat16

EPS = 1e-6
LOG2E = 1.4426950408889634
D_MODEL = 1024
LANES = 128
BF16_SUBLANES = 16
ATTN_WIDTH = 512
ATTN_HEAD_DIM = 64
ATTN_GROUP = 4
ATTN_KV_HEADS = 2
ATTN_KV_WIDTH = ATTN_KV_HEADS * ATTN_HEAD_DIM
ATTN_BLOCK = 128
HGRN_HEADS = 4
HGRN_DIM = 128
HGRN_WIDTH = HGRN_HEADS * HGRN_DIM
HGRN_PAIR = 2 * HGRN_DIM
COL_Q = 0
COL_K = COL_Q + ATTN_WIDTH
COL_V = COL_K + ATTN_KV_WIDTH
COL_HQ = COL_V + ATTN_KV_WIDTH
COL_HF = COL_HQ + HGRN_WIDTH
COL_HI = COL_HF + HGRN_WIDTH
COL_HG = COL_HI + HGRN_WIDTH
IN_PROJ_WIDTH = COL_HG + HGRN_WIDTH
IN_PROJ_STEP = 512
OUT_PROJ_STEP = 512
HGRN_CHUNK = 64
HGRN_SUB = 16
HGRN_NSUB = HGRN_CHUNK // HGRN_SUB
CA_HEADS = 4
CA_HEAD_DIM = 256
D_FF = 2816
FF_CHUNK = 256
FF_DOWN_GROUP = 6
HALO = 8

TM_MIX = 512
TS_MIX = 256
TM_CA = 1024
TM_FFN = 512

VMEM_LIMIT = 56 * 1024 * 1024

NT_DIMS = (((1,), (1,)), ((), ()))


def _rms(x, w):
    return x * lax.rsqrt(jnp.mean(x * x, axis=-1, keepdims=True) + EPS) * w


def _dot(a, b):
    return jnp.dot(a, b, preferred_element_type=F32)


def _dot_nt(a, b):
    return lax.dot_general(a, b, NT_DIMS, preferred_element_type=F32)


def _swa_scores(z_ref, kv_ref, r_ref, row0, blk, j):
    nq = ATTN_BLOCK
    gw = ATTN_GROUP * ATTN_HEAD_DIM
    r0 = blk * nq
    q = z_ref[r0:r0 + nq, COL_Q + j * gw:COL_Q + (j + 1) * gw] * (ATTN_HEAD_DIM ** -0.5 * LOG2E)
    kk = kv_ref[r0:r0 + 2 * nq, 0:ATTN_KV_WIDTH]
    k_sw = pltpu.roll(kk, ATTN_HEAD_DIM, axis=1)
    low = lax.broadcasted_iota(jnp.int32, kk.shape, 1) < ATTN_HEAD_DIM
    kj = jnp.where(low, kk, k_sw) if j == 0 else jnp.where(low, k_sw, kk)
    k_rep = jnp.concatenate([kj, kj], axis=1).astype(BF16)
    head_of_lane = lax.broadcasted_iota(jnp.int32, (nq, gw), 1) // ATTN_HEAD_DIM
    qs = jnp.concatenate([jnp.where(head_of_lane == h, q, 0.0) for h in range(ATTN_GROUP)], axis=0).astype(BF16)
    r_ref[row0:row0 + ATTN_GROUP * nq, :] = _dot_nt(qs, k_rep)


def _swa_finish(r_ref, s_row0, o_row0, kv_ref, sink_ref, mix_ref, blk, j, first_block):
    nq = ATTN_BLOCK
    s = r_ref[s_row0:s_row0 + ATTN_GROUP * nq, :]
    gw = ATTN_GROUP * ATTN_HEAD_DIM
    r0 = blk * nq
    qi = lax.broadcasted_iota(jnp.int32, (nq, 2 * nq), 0)
    kj = lax.broadcasted_iota(jnp.int32, (nq, 2 * nq), 1)
    diff = qi + nq - kj
    allowed = (diff >= 0) & (diff < nq) & ((kj >= nq) | jnp.logical_not(first_block))
    ps, denoms = [], []
    for h in range(ATTN_GROUP):
        sink = sink_ref[j * ATTN_GROUP + h] * LOG2E
        sh = jnp.where(allowed, s[h * nq:(h + 1) * nq], -jnp.inf)
        mh = jnp.maximum(jnp.max(sh, axis=-1, keepdims=True), sink)
        ph = jnp.exp2(sh - mh)
        ps.append(ph.astype(BF16))
        denoms.append(jnp.sum(ph, axis=-1, keepdims=True) + jnp.exp2(sink - mh))
    p = jnp.concatenate(ps, axis=0)

    vv = kv_ref[r0:r0 + 2 * nq, ATTN_KV_WIDTH:2 * ATTN_KV_WIDTH]
    v_sw = pltpu.roll(vv, ATTN_HEAD_DIM, axis=1)
    low = lax.broadcasted_iota(jnp.int32, vv.shape, 1) < ATTN_HEAD_DIM
    vj = jnp.where(low, vv, v_sw) if j == 0 else jnp.where(low, v_sw, vv)
    v_rep = jnp.concatenate([vj, vj], axis=1).astype(BF16)
    r_ref[o_row0:o_row0 + ATTN_GROUP * nq, :] = _dot(p, v_rep)
    head_of_lane = lax.broadcasted_iota(jnp.int32, (nq, gw), 1) // ATTN_HEAD_DIM
    og = None
    for h in range(ATTN_GROUP):
        oh = r_ref[o_row0 + h * nq:o_row0 + (h + 1) * nq, :] * (1.0 / denoms[h])
        oh = jnp.where(head_of_lane == h, oh, 0.0)
        og = oh if og is None else og + oh
    mix_ref[r0:r0 + nq, j * gw:(j + 1) * gw] = og.astype(BF16)


def _hgrn_gates(z_ref, lb, q_ref, k_ref, ghl_ref, rows):
    hq = z_ref[rows, COL_HQ:COL_HQ + HGRN_WIDTH] * (0.5 * HGRN_DIM ** -0.5)
    q_ref[rows, :] = hq + hq * jnp.tanh(z_ref[rows, COL_HQ:COL_HQ + HGRN_WIDTH] * 0.5)
    t = jnp.tanh(z_ref[rows, COL_HF:COL_HF + HGRN_WIDTH] * 0.5)
    half_span = 0.5 * (1.0 - lb)
    f = (1.0 - half_span) + half_span * t
    k_ref[rows, :] = half_span - half_span * t
    g = jnp.log(f) * LOG2E
    g_hi = g.astype(BF16)
    ghl_ref[rows, 0:HGRN_WIDTH] = g_hi
    ghl_ref[rows, HGRN_WIDTH:2 * HGRN_WIDTH] = (g - g_hi.astype(F32)).astype(BF16)


def _hgrn_cumsum(ghl_ref, cs_ref, bc_ref):
    tm = bc_ref.shape[0]
    grp = min(tm, 2 * LANES)
    ti = lax.broadcasted_iota(jnp.int32, (grp, grp), 0)
    si = lax.broadcasted_iota(jnp.int32, (grp, grp), 1)
    same_chunk = jnp.bitwise_xor(ti, si) < HGRN_CHUNK
    tri = ((si <= ti) & same_chunk).astype(BF16)
    for r0 in range(0, tm, grp):
        cs_ref[r0:r0 + grp, :] = _dot(tri, ghl_ref[r0:r0 + grp, :])
    bc_ref[...] = cs_ref[:, 0:HGRN_WIDTH] + cs_ref[:, HGRN_WIDTH:2 * HGRN_WIDTH]


def _hgrn_factors(c, q_ref, k_ref, bc_ref, qe_ref, kecat_ref, qd_ref, kd_ref):
    r0c = c * HGRN_CHUNK
    rows = slice(r0c, r0c + HGRN_CHUNK)
    bc = bc_ref[rows, :]
    q = q_ref[rows, :]
    k = k_ref[rows, :]
    for i in range(HGRN_NSUB):
        r0 = i * HGRN_SUB
        anchor = bc[r0 + HGRN_SUB // 2 - 1:r0 + HGRN_SUB // 2, :]
        qe_ref[r0c + r0:r0c + r0 + HGRN_SUB, :] = (q[r0:r0 + HGRN_SUB] * jnp.exp2(bc[r0:r0 + HGRN_SUB] - anchor)
                                                   ).astype(BF16)
        n_valid = r0 + HGRN_SUB
        ke = k[0:n_valid] * jnp.exp2(anchor - bc[0:n_valid])
        kecat_ref[c, i * HGRN_CHUNK:i * HGRN_CHUNK + n_valid, :] = ke.astype(BF16)
    qd_ref[rows, :] = (q * jnp.exp2(bc)).astype(BF16)
    b_last = bc[HGRN_CHUNK - 1:HGRN_CHUNK, :]
    e = c % 2
    kd_ref[c, e * HGRN_CHUNK:(e + 1) * HGRN_CHUNK, :] = (k * jnp.exp2(b_last - bc)).astype(BF16)
    kd_ref[c, (1 - e) * HGRN_CHUNK:(2 - e) * HGRN_CHUNK, :] = jnp.zeros((HGRN_CHUNK, HGRN_WIDTH), BF16)
    return jnp.exp2(b_last)


def _hgrn_scores(c, qe_ref, kecat_ref, r_ref, row0):
    rows = slice(c * HGRN_CHUNK, (c + 1) * HGRN_CHUNK)
    for h in range(HGRN_HEADS):
        hs = slice(h * HGRN_DIM, (h + 1) * HGRN_DIM)
        r_ref[row0 + h * HGRN_CHUNK:row0 + (h + 1) * HGRN_CHUNK, :] = _dot_nt(qe_ref[rows, hs], kecat_ref[c, :, hs])


def _hgrn_scores_finish(c, r_ref, row0, a_ref):
    rows = slice(c * HGRN_CHUNK, (c + 1) * HGRN_CHUNK)
    t = lax.broadcasted_iota(jnp.int32, (HGRN_CHUNK, LANES), 0)
    l = lax.broadcasted_iota(jnp.int32, (HGRN_CHUNK, LANES), 1)
    keep = ((l // HGRN_CHUNK) == ((t // HGRN_SUB) % 2)) & ((l % HGRN_CHUNK) <= t)
    half = HGRN_CHUNK // 2
    for h in range(HGRN_HEADS):
        hs = slice(h * HGRN_DIM, (h + 1) * HGRN_DIM)
        pr = row0 + h * HGRN_CHUNK
        a = jnp.concatenate([r_ref[pr:pr + half, 0:LANES], r_ref[pr + half:pr + HGRN_CHUNK, LANES:2 * LANES]], axis=0)
        a_ref[rows, hs] = jnp.where(keep, a, 0.0).astype(BF16)


def _pair_diag_mask():
    r = lax.broadcasted_iota(jnp.int32, (HGRN_PAIR, HGRN_PAIR), 0)
    l = lax.broadcasted_iota(jnp.int32, (HGRN_PAIR, HGRN_PAIR), 1)
    return (r // HGRN_DIM) == (l // HGRN_DIM)


def _hgrn_transpose_v(z_ref, vt_ref):
    eye = (lax.broadcasted_iota(jnp.int32, (LANES, LANES), 0)
           == lax.broadcasted_iota(jnp.int32, (LANES, LANES), 1)).astype(BF16)
    for a in range(HGRN_WIDTH // LANES):
        v = z_ref[:, COL_HI + a * LANES:COL_HI + (a + 1) * LANES].astype(BF16)
        vt_ref[a * LANES:(a + 1) * LANES, :] = _dot_nt(eye, v).astype(BF16)


def _hgrn_states(n_chunks, vt_ref, kd_ref, st_ref, stbf_ref, decays, r_ref, row0):
    for p in range(HGRN_HEADS // 2):
        ps = slice(p * HGRN_PAIR, (p + 1) * HGRN_PAIR)
        for c in range(n_chunks):
            tok = slice((c // 2) * LANES, (c // 2 + 1) * LANES)
            ur = row0 + (p * n_chunks + c) * HGRN_PAIR
            r_ref[ur:ur + HGRN_PAIR, :] = _dot(vt_ref[ps, tok], kd_ref[c, :, ps])
    for h in range(HGRN_HEADS):
        p, blk = h // 2, slice((h % 2) * HGRN_DIM, (h % 2 + 1) * HGRN_DIM)
        st = st_ref[h]
        for c in range(n_chunks):
            ur = row0 + (p * n_chunks + c) * HGRN_PAIR + (h % 2) * HGRN_DIM
            stbf_ref[c, p, blk, blk] = st.astype(BF16)
            st = st * decays[c][:, h * HGRN_DIM:(h + 1) * HGRN_DIM] + r_ref[ur:ur + HGRN_DIM, blk]
        st_ref[h] = st


def _hgrn_outputs(c, z_ref, a_ref, qd_ref, stbf_ref, onw, mix_ref, r_ref, row0):
    rows = slice(c * HGRN_CHUNK, (c + 1) * HGRN_CHUNK)
    diag = _pair_diag_mask()
    for p in range(HGRN_HEADS // 2):
        ps = slice(p * HGRN_PAIR, (p + 1) * HGRN_PAIR)
        vp = z_ref[rows, COL_HI + p * HGRN_PAIR:COL_HI + (p + 1) * HGRN_PAIR].astype(BF16)
        v_bd = jnp.where(diag, jnp.concatenate([vp] * (HGRN_PAIR // HGRN_CHUNK), axis=0), 0.0)
        orow = row0 + p * HGRN_CHUNK
        r_ref[orow:orow + HGRN_CHUNK, :] = _dot(a_ref[rows, ps], v_bd) + _dot_nt(qd_ref[rows, ps], stbf_ref[c, p])
        o = r_ref[orow:orow + HGRN_CHUNK, :]
        gr = z_ref[rows, COL_HG + p * HGRN_PAIR:COL_HG + (p + 1) * HGRN_PAIR]
        hg = 0.5 * gr
        gate = hg + hg * jnp.tanh(hg)
        outs = []
        for hh in range(2):
            oh = o[:, hh * HGRN_DIM:(hh + 1) * HGRN_DIM]
            outs.append(oh * lax.rsqrt(jnp.mean(oh * oh, axis=-1, keepdims=True) + EPS) * onw)
        rec = jnp.concatenate(outs, axis=1) * gate
        mix_ref[rows, ATTN_WIDTH + p * HGRN_PAIR:ATTN_WIDTH + (p + 1) * HGRN_PAIR] = rec.astype(BF16)


def _mixer_kernel(n_cast, x_ref, prew_ref, win_ref, sink_ref, lbl_ref, onw_ref, wout_ref, postw_ref, *rest):
    cast_in, o_ref, cast_out = rest[:n_cast], rest[n_cast], rest[n_cast + 1:2 * n_cast + 1]
    (z_ref, kv_ref, st_ref, mix_ref, q_ref, k_ref, ghl_ref, bc_ref, qe_ref, kecat_ref, qd_ref, kd_ref,
     a_ref, vt_ref, stbf_ref, cs_ref, r_ref, h_ref, m_ref) = rest[2 * n_cast + 1:]
    for src, dst in zip(cast_in, cast_out):
        dst[...] = src[...].astype(BF16)
    n = pl.program_id(1)
    tm = x_ref.shape[1]
    ts = z_ref.shape[1]
    n_sub = tm // ts
    n_chunks = ts // HGRN_CHUNK
    units = [(blk, j) for blk in range(ts // ATTN_BLOCK) for j in range(ATTN_KV_HEADS)]

    @pl.when(n == 0)
    def _():
        kv_ref[0:ATTN_BLOCK, :] = jnp.zeros((ATTN_BLOCK, 2 * ATTN_KV_WIDTH), F32)
        st_ref[...] = jnp.zeros_like(st_ref)
        kecat_ref[...] = jnp.zeros_like(kecat_ref)
        stbf_ref[...] = jnp.zeros_like(stbf_ref)

    l0 = lbl_ref[0:1, :]
    l1 = lbl_ref[1:2, :]
    lm = jnp.maximum(l0, l1)
    e0 = jnp.exp(l0 - lm)
    lb = e0 / (e0 + jnp.exp(l1 - lm))

    unit_rows = ATTN_GROUP * ATTN_BLOCK
    s_base = 0
    pv_base = s_base + len(units) * unit_rows
    p_base = pv_base + len(units) * unit_rows
    upd_base = p_base + n_chunks * HGRN_HEADS * HGRN_CHUNK
    o_base = upd_base + n_chunks * HGRN_HEADS // 2 * HGRN_PAIR
    assert o_base + n_chunks * HGRN_HEADS // 2 * HGRN_CHUNK == r_ref.shape[0]

    def p_rows(c):
        return p_base + c * HGRN_HEADS * HGRN_CHUNK

    def o_rows(c):
        return o_base + c * HGRN_HEADS // 2 * HGRN_CHUNK

    def project_in(s):
        rows = slice(s * ts, (s + 1) * ts)
        z = z_ref.at[s % 2]

        def norm():
            h_ref[s % 2] = _rms(x_ref[0, rows, :], prew_ref[...]).astype(BF16)

        def cols(c0, c1):
            def step():
                z[:, c0:c1] = _dot(h_ref[s % 2], win_ref[:, c0:c1])
            return step

        def keep_kv():
            kv_ref[ATTN_BLOCK + s * ts:ATTN_BLOCK + (s + 1) * ts, :] = z[:, COL_K:COL_HQ]

        bounds = list(range(COL_HQ, IN_PROJ_WIDTH, IN_PROJ_STEP)) + [IN_PROJ_WIDTH]
        hgrn_steps = [norm] + [cols(a, b) for a, b in zip(bounds[:-1], bounds[1:])]
        bounds = list(range(0, COL_HQ, IN_PROJ_STEP)) + [COL_HQ]
        return hgrn_steps, [cols(a, b) for a, b in zip(bounds[:-1], bounds[1:])] + [keep_kv]

    def project_out(s):
        rows = slice(s * ts, (s + 1) * ts)

        def cols(c0, c1):
            def step():
                m_ref[s % 2, :, c0:c1] = _dot(mix_ref[s % 2], wout_ref[:, c0:c1])
            return step

        def finish():
            o_ref[0, rows, :] = x_ref[0, rows, :] + _rms(m_ref[s % 2], postw_ref[...])

        return [cols(c0, c0 + OUT_PROJ_STEP) for c0 in range(0, D_MODEL, OUT_PROJ_STEP)] + [finish]

    def mix_tokens(s, fill, fill_own):
        z, mix = z_ref.at[s % 2], mix_ref.at[s % 2]
        q, k, ghl, bc, qe, kecat, qd, kd = q_ref, k_ref, ghl_ref, bc_ref, qe_ref, kecat_ref, qd_ref, kd_ref
        a, vt, stbf, cs, r = a_ref, vt_ref, stbf_ref, cs_ref, r_ref
        kv = kv_ref.at[pl.ds(s * ts, ATTN_BLOCK + ts)]

        def first_block(blk):
            return (n == 0) if (s == 0 and blk == 0) else jnp.bool_(False)

        fill(2)
        for c in range(n_chunks):
            _hgrn_gates(z, lb, q, k, ghl, slice(c * HGRN_CHUNK, (c + 1) * HGRN_CHUNK))
            if c % 2 == 1:
                fill()
        _hgrn_transpose_v(z, vt)
        _hgrn_cumsum(ghl, cs, bc)
        fill_own()
        _swa_scores(z, kv, r, s_base, *units[0])
        decays = []
        for c in range(n_chunks):
            fill()
            decays.append(_hgrn_factors(c, q, k, bc, qe, kecat, qd, kd))
            _hgrn_scores(c, qe, kecat, r, p_rows(c))
        fill()
        _hgrn_states(n_chunks, vt, kd, st_ref, stbf, decays, r, upd_base)
        for c in range(n_chunks):
            _hgrn_scores_finish(c, r, p_rows(c), a)
        for u, (blk, j) in enumerate(units):
            fill()
            if u + 1 < len(units):
                _swa_scores(z, kv, r, s_base + (u + 1) * unit_rows, *units[u + 1])
            if u < n_chunks:
                _hgrn_outputs(u, z, a, qd, stbf, onw_ref[...], mix, r, o_rows(u))
            _swa_finish(r, s_base + u * unit_rows, pv_base + u * unit_rows, kv, sink_ref, mix, blk, j,
                        first_block(blk))
        for c in range(len(units), n_chunks):
            _hgrn_outputs(c, z, a, qd, stbf, onw_ref[...], mix, r, o_rows(c))
        fill(None)

    hgrn_steps, own = project_in(0)
    for step in hgrn_steps:
        step()
    for s in range(n_sub):
        nxt, nxt_own = project_in(s + 1) if s + 1 < n_sub else ([], [])
        prv = project_out(s - 1) if s > 0 else []
        pending = list(own)
        while nxt or prv:
            if nxt:
                pending.append(nxt.pop(0))
            if prv:
                pending.append(prv.pop(0))
        own_left = [len(own)]

        def fill(count=1):
            for _ in range(len(pending) if count is None else min(count, len(pending))):
                pending.pop(0)()
                own_left[0] -= 1

        def fill_own():
            fill(max(own_left[0], 0))

        mix_tokens(s, fill, fill_own)
        own = nxt_own
    for step in project_out(n_sub - 1):
        step()
    kv_ref[0:ATTN_BLOCK, :] = kv_ref[tm:tm + ATTN_BLOCK, :]


def _mixer_result_rows(tm):
    units = (tm // ATTN_BLOCK) * ATTN_KV_HEADS
    chunks = tm // HGRN_CHUNK
    return (2 * units * ATTN_GROUP * ATTN_BLOCK + chunks * HGRN_HEADS * HGRN_CHUNK
            + chunks * (HGRN_HEADS // 2) * (HGRN_PAIR + HGRN_CHUNK))


def _cast_block_spec(w, b, nt):
    rows, cols = w.shape
    assert rows % (b * nt * BF16_SUBLANES) == 0
    return pl.BlockSpec((rows // (b * nt), cols), lambda i, j: (i * nt + j, 0))


def _mixer(x, pre_w, w_in, sinks, lb_logits, out_norm_w, w_out, post_w, later_weights):
    b, t, d = x.shape
    tm, ts = TM_MIX, TS_MIX
    ns = tm // ts
    nt = t // tm
    assert t % tm == 0 and d == D_MODEL and w_in.shape == (D_MODEL, IN_PROJ_WIDTH)
    assert tm % ts == 0 and ts % (2 * LANES) == 0
    full = lambda shape: pl.BlockSpec(shape, lambda i, j: (0,) * len(shape), pipeline_mode=pl.Buffered(1))
    cast_specs = [_cast_block_spec(w, b, nt) for w in later_weights]
    outs = pl.pallas_call(
        functools.partial(_mixer_kernel, len(later_weights)),
        grid=(b, nt),
        in_specs=[
            pl.BlockSpec((1, tm, d), lambda i, j: (i, j, 0)),
            full((1, d)),
            full(w_in.shape),
            pl.BlockSpec(memory_space=pltpu.SMEM),
            full(lb_logits.shape),
            full((1, HGRN_DIM)),
            full(w_out.shape),
            full((1, d)),
        ] + cast_specs,
        out_specs=[pl.BlockSpec((1, tm, d), lambda i, j: (i, j, 0))] + cast_specs,
        out_shape=[jax.ShapeDtypeStruct((b, t, d), F32)] + [jax.ShapeDtypeStruct(w.shape, BF16) for w in later_weights],
        scratch_shapes=[
            pltpu.VMEM((2, ts, IN_PROJ_WIDTH), F32),
            pltpu.VMEM((ATTN_BLOCK + tm, 2 * ATTN_KV_WIDTH), F32),
            pltpu.VMEM((HGRN_HEADS, HGRN_DIM, HGRN_DIM), F32),
            pltpu.VMEM((2, ts, ATTN_WIDTH + HGRN_WIDTH), BF16),
            pltpu.VMEM((ts, HGRN_WIDTH), F32),
            pltpu.VMEM((ts, HGRN_WIDTH), F32),
            pltpu.VMEM((ts, 2 * HGRN_WIDTH), BF16),
            pltpu.VMEM((ts, HGRN_WIDTH), F32),
            pltpu.VMEM((ts, HGRN_WIDTH), BF16),
            pltpu.VMEM((ts // HGRN_CHUNK, HGRN_NSUB * HGRN_CHUNK, HGRN_WIDTH), BF16),
            pltpu.VMEM((ts, HGRN_WIDTH), BF16),
            pltpu.VMEM((ts // HGRN_CHUNK, 2 * HGRN_CHUNK, HGRN_WIDTH), BF16),
            pltpu.VMEM((ts, HGRN_WIDTH), BF16),
            pltpu.VMEM((HGRN_WIDTH, ts), BF16),
            pltpu.VMEM((ts // HGRN_CHUNK, HGRN_HEADS // 2, HGRN_PAIR, HGRN_PAIR), BF16),
            pltpu.VMEM((ts, 2 * HGRN_WIDTH), F32),
            pltpu.VMEM((_mixer_result_rows(ts), HGRN_PAIR), F32),
            pltpu.VMEM((2, ts, D_MODEL), BF16),
            pltpu.VMEM((2, ts, D_MODEL), F32),
        ],
        compiler_params=pltpu.CompilerParams(
            dimension_semantics=("arbitrary", "arbitrary"), vmem_limit_bytes=VMEM_LIMIT),
        name="mixer",
    )(x, pre_w, w_in, sinks, lb_logits, out_norm_w, w_out, post_w, *later_weights)
    return outs[0], outs[1:]


def _xattn_kernel(x_ref, prew_ref, wq_ref, mem_ref, memw_ref, wk_ref, wv_ref, wo_ref, postw_ref, o_ref, cat_ref,
                  k_ref, v_ref):
    @pl.when(pl.program_id(1) == 0)
    def _():
        mn = _rms(mem_ref[0], memw_ref[...]).astype(BF16)
        k_ref[0] = _dot(mn, wk_ref[...]).astype(BF16)
        v_ref[0] = _dot(mn, wv_ref[...]).astype(BF16)

    x = x_ref[0]
    h = _rms(x, prew_ref[...]).astype(BF16)
    q = (_dot(h, wq_ref[...]) * (CA_HEAD_DIM ** -0.5 * LOG2E)).astype(BF16)

    def scores(hd):
        hs = slice(hd * CA_HEAD_DIM, (hd + 1) * CA_HEAD_DIM)
        return _dot_nt(q[:, hs], k_ref[0, :, hs])

    s_next = scores(0)
    for hd in range(CA_HEADS):
        hs = slice(hd * CA_HEAD_DIM, (hd + 1) * CA_HEAD_DIM)
        s = s_next
        if hd + 1 < CA_HEADS:
            s_next = scores(hd + 1)
        m = jnp.max(s, axis=-1, keepdims=True)
        p = jnp.exp2(s - m)
        p = (p * (1.0 / jnp.sum(p, axis=-1, keepdims=True))).astype(BF16)
        cat_ref[:, hs] = _dot(p, v_ref[0, :, hs]).astype(BF16)
    half = x.shape[0] // 2
    for rows in (slice(0, half), slice(half, 2 * half)):
        c = _dot(cat_ref[rows, :], wo_ref[...])
        o_ref[0, rows, :] = x[rows] + _rms(c, postw_ref[...])


def _xattn(x, pre_w, wq, mem, mem_w, wk, wv, wo, post_w):
    b, t, d = x.shape
    tm = TM_CA
    nm = mem.shape[1]
    assert t % tm == 0
    full = lambda shape: pl.BlockSpec(shape, lambda i, j: (0,) * len(shape), pipeline_mode=pl.Buffered(1))
    return pl.pallas_call(
        _xattn_kernel,
        grid=(b, t // tm),
        in_specs=[
            pl.BlockSpec((1, tm, d), lambda i, j: (i, j, 0)),
            full((1, d)),
            full(wq.shape),
            pl.BlockSpec((1, nm, d), lambda i, j: (i, 0, 0)),
            full((1, d)),
            full(wk.shape),
            full(wv.shape),
            full(wo.shape),
            full((1, d)),
        ],
        out_specs=pl.BlockSpec((1, tm, d), lambda i, j: (i, j, 0)),
        out_shape=jax.ShapeDtypeStruct((b, t, d), F32),
        scratch_shapes=[pltpu.VMEM((tm, d), BF16), pltpu.VMEM((1, nm, d), BF16), pltpu.VMEM((1, nm, d), BF16)],
        compiler_params=pltpu.CompilerParams(
            dimension_semantics=("arbitrary", "arbitrary"), vmem_limit_bytes=VMEM_LIMIT),
        name="xattn",
    )(x, pre_w, wq, mem, mem_w, wk, wv, wo, post_w)


def _gelu_tanh(x):
    return 0.5 * x * (1.0 + jnp.tanh(0.7978845608028654 * (x + 0.044715 * (x * x * x))))


def _causal_conv3(u_ref, kb, tm, cw_ref, cb_ref):
    ls = slice(kb * LANES, (kb + 1) * LANES)
    out = (cw_ref[2:3, ls] * u_ref[kb, HALO:HALO + tm, :] + cw_ref[1:2, ls] * u_ref[kb, HALO - 1:HALO - 1 + tm, :]
           + cw_ref[0:1, ls] * u_ref[kb, HALO - 2:HALO - 2 + tm, :]) + cb_ref[:, ls]
    u_ref[kb, 0:HALO, :] = u_ref[kb, tm:tm + HALO, :]
    return out


def _ffn_kernel(x_ref, prew_ref, wup_ref, cw_ref, cb_ref, wdn_ref, postw_ref, o_ref, u_ref, act_ref):
    n = pl.program_id(1)
    tm = x_ref.shape[1]
    n_chunks = D_FF // FF_CHUNK
    kb_per_chunk = FF_CHUNK // LANES
    kb_val = D_FF // LANES

    @pl.when(n == 0)
    def _():
        u_ref[:, 0:HALO, :] = jnp.zeros((u_ref.shape[0], HALO, LANES), F32)

    x = x_ref[0]
    h = _rms(x, prew_ref[...]).astype(BF16)

    def up(c):
        for base, col0 in ((0, 0), (kb_val, D_FF)):
            u = _dot(h, wup_ref[:, col0 + c * FF_CHUNK:col0 + (c + 1) * FF_CHUNK])
            for j in range(kb_per_chunk):
                u_ref[base + c * kb_per_chunk + j, HALO:HALO + tm, :] = u[:, j * LANES:(j + 1) * LANES]

    acc = None
    group_start = 0
    up(0)
    for c in range(n_chunks):
        if c + 1 < n_chunks:
            up(c + 1)
        for j in range(kb_per_chunk):
            kb = c * kb_per_chunk + j
            gate = _causal_conv3(u_ref, kb, tm, cw_ref, cb_ref)
            val = _causal_conv3(u_ref, kb_val + kb, tm, cw_ref, cb_ref)
            act_ref[:, kb * LANES:(kb + 1) * LANES] = (_gelu_tanh(gate) * val).astype(BF16)
        if (c + 1) % FF_DOWN_GROUP == 0 or c + 1 == n_chunks:
            ks = slice(group_start * FF_CHUNK, (c + 1) * FF_CHUNK)
            part = _dot(act_ref[:, ks], wdn_ref[ks, :])
            acc = part if acc is None else acc + part
            group_start = c + 1
    o_ref[0] = x + _rms(acc, postw_ref[...])


def _ffn(x, pre_w, w_up, conv_w, conv_b, w_down, post_w):
    b, t, d = x.shape
    tm = TM_FFN
    assert t % tm == 0 and w_up.shape == (D_MODEL, 2 * D_FF) and D_FF % FF_CHUNK == 0
    full = lambda shape: pl.BlockSpec(shape, lambda i, j: (0,) * len(shape), pipeline_mode=pl.Buffered(1))
    return pl.pallas_call(
        _ffn_kernel,
        grid=(b, t // tm),
        in_specs=[
            pl.BlockSpec((1, tm, d), lambda i, j: (i, j, 0)),
            full((1, d)),
            full(w_up.shape),
            full(conv_w.shape),
            full((1, 2 * D_FF)),
            full(w_down.shape),
            full((1, d)),
        ],
        out_specs=pl.BlockSpec((1, tm, d), lambda i, j: (i, j, 0)),
        out_shape=jax.ShapeDtypeStruct((b, t, d), F32),
        scratch_shapes=[pltpu.VMEM((2 * D_FF // LANES, HALO + tm, LANES), F32), pltpu.VMEM((tm, D_FF), BF16)],
        compiler_params=pltpu.CompilerParams(
            dimension_semantics=("arbitrary", "arbitrary"), vmem_limit_bytes=VMEM_LIMIT),
        name="ffn",
    )(x, pre_w, w_up, conv_w, conv_b, w_down, post_w)


def kernel(x, mem, mix_pre_norm, w_in, attn_sinks, hgrn_lb_logits, hgrn_out_norm, w_out, mix_post_norm, ca_pre_norm,
           mem_norm, ca_wq, ca_wk, ca_wv, ca_wo, ca_post_norm, ffn_pre_norm, ffn_w_up, ffn_conv_w, ffn_conv_b,
           ffn_w_down, ffn_post_norm):
    depth = w_in.shape[0]
    assert depth == 1 and hgrn_lb_logits.shape[0] == 2
    bf = lambda w: w.astype(BF16)
    for l in range(depth):
        later = (ca_wq[l], ca_wk[l], ca_wv[l], ca_wo[l], ffn_w_up[l])
        x, (wq, wk, wv, wo, w_up) = _mixer(
            x, mix_pre_norm[l:l + 1], bf(w_in[l]), attn_sinks[l], hgrn_lb_logits, hgrn_out_norm[l:l + 1],
            bf(w_out[l]), mix_post_norm[l:l + 1], later)
        x = _xattn(x, ca_pre_norm[l:l + 1], wq, mem, mem_norm[l:l + 1], wk, wv, wo, ca_post_norm[l:l + 1])
        x = _ffn(x, ffn_pre_norm[l:l + 1], w_up, ffn_conv_w[l], ffn_conv_b[l:l + 1], bf(ffn_w_down[l]),
                 ffn_post_norm[l:l + 1])
    return x
```
